```python
import math
import jax, jax.numpy as jnp
from jax import lax
import numpy as np

D_MODEL = 2048
BATCH = 2
SEQ = 8192
DEPTH = 1

HEAD_DIM = 128
SB_HEADS = 8
MOBA_HEADS = 8
SB_WIDTH = SB_HEADS * HEAD_DIM
MOBA_WIDTH = MOBA_HEADS * HEAD_DIM
Q_BLOCK = 128
MOBA_BLOCK = 256
MOBA_TOPK = 3
MOBA_Q_CHUNK = 32
REL_BUCKETS = 32
REL_MAX_DIST = 1024
PEER_HEADS = 8
PEER_NKEYS = 128
PEER_N_EXPERTS = PEER_NKEYS * PEER_NKEYS
PEER_QDIM = 256
PEER_TOPK = 16
PEER_TOK_CHUNK = 128
LN_EPS = 1e-5
DN_ALPHA = (2.0 * DEPTH) ** 0.25
DN_BETA = (8.0 * DEPTH) ** -0.25
NEG = -1e30

kernel_name = "hybrid_sb_moba_peer_deepnorm"


def layer_norm(x, g, b):
    xf = x.astype(jnp.float32)
    mu = jnp.mean(xf, axis=-1, keepdims=True)
    var = jnp.mean(jnp.square(xf - mu), axis=-1, keepdims=True)
    y = (xf - mu) * lax.rsqrt(var + LN_EPS)
    return (y * g.astype(jnp.float32) + b.astype(jnp.float32)).astype(x.dtype)


def rel_bucket(rel):
    n_exact = REL_BUCKETS // 2
    rel = jnp.maximum(rel, 0)
    logd = jnp.log(jnp.maximum(rel, 1).astype(jnp.float32) / n_exact) / math.log(REL_MAX_DIST / n_exact)
    large = n_exact + (logd * (REL_BUCKETS - n_exact)).astype(jnp.int32)
    large = jnp.minimum(large, REL_BUCKETS - 1)
    return jnp.where(rel < n_exact, rel, large)


def stick_breaking_attention(q, k, v):
    B, H, S, dh = q.shape
    nblk = S // Q_BLOCK
    scale = dh ** -0.5
    qb = q.reshape(B, H, nblk, Q_BLOCK, dh).transpose(2, 0, 1, 3, 4)
    key_pos = jnp.arange(S)

    def block(args):
        qi, i = args
        z = jnp.einsum('bhqd,bhkd->bhqk', qi, k, preferred_element_type=jnp.float32) * scale
        q_pos = i * Q_BLOCK + jnp.arange(Q_BLOCK)
        causal = key_pos[None, :] < q_pos[:, None]
        log_1mb = jnp.where(causal, jax.nn.log_sigmoid(-z), 0.0)
        tail = lax.cumsum(log_1mb, axis=3, reverse=True) - log_1mb
        w = jnp.where(causal, jnp.exp(jax.nn.log_sigmoid(z) + tail), 0.0)
        return jnp.einsum('bhqk,bhkd->bhqd', w.astype(v.dtype), v)

    out = lax.map(block, (qb, jnp.arange(nblk)))
    return out.transpose(1, 2, 0, 3, 4).reshape(B, H, S, dh)


def moba_attention(q, k, v, rel_bias):
    B, H, S, dh = q.shape
    nb = S // MOBA_BLOCK
    ksel = min(MOBA_TOPK, nb)
    nchunk = S // MOBA_Q_CHUNK
    scale = dh ** -0.5
    kb = k.reshape(B, H, nb, MOBA_BLOCK, dh)
    vb = v.reshape(B, H, nb, MOBA_BLOCK, dh)
    k_mean = jnp.mean(kb.astype(jnp.float32), axis=3)
    qc = q.reshape(B, H, nchunk, MOBA_Q_CHUNK, dh).transpose(2, 0, 1, 3, 4)
    b_idx = jnp.arange(B)[:, None, None, None]
    h_idx = jnp.arange(H)[None, :, None, None]
    h5 = jnp.arange(H)[None, :, None, None, None]
    offs = jnp.arange(MOBA_BLOCK)
    blk_ids = jnp.arange(nb)

    def chunk(args):
        qi, c = args
        q_pos = c * MOBA_Q_CHUNK + jnp.arange(MOBA_Q_CHUNK)
        own = (c * MOBA_Q_CHUNK) // MOBA_BLOCK
        gate = jnp.einsum('bhqd,bhnd->bhqn', qi.astype(jnp.float32), k_mean)
        gate = jnp.where(blk_ids < own, gate, NEG)
        _, sel = lax.top_k(gate, ksel)
        sel_valid = jnp.arange(ksel) < own
        sel = jnp.where(sel_valid, sel, 0)
        k_sel = kb[b_idx, h_idx, sel]
        v_sel = vb[b_idx, h_idx, sel]
        s_sel = jnp.einsum('bhqd,bhqrkd->bhqrk', qi, k_sel, preferred_element_type=jnp.float32) * scale
        pos_sel = sel[..., None] * MOBA_BLOCK + offs
        bucket_sel = rel_bucket(q_pos[None, None, :, None, None] - pos_sel)
        s_sel = s_sel + rel_bias[h5, bucket_sel].astype(jnp.float32)
        s_sel = jnp.where(sel_valid[:, None], s_sel, NEG)
        k_own = lax.dynamic_index_in_dim(kb, own, axis=2, keepdims=False)
        v_own = lax.dynamic_index_in_dim(vb, own, axis=2, keepdims=False)
        own_pos = own * MOBA_BLOCK + offs
        s_own = jnp.einsum('bhqd,bhkd->bhqk', qi, k_own, preferred_element_type=jnp.float32) * scale
        s_own = s_own + rel_bias[:, rel_bucket(q_pos[:, None] - own_pos[None, :])][None].astype(jnp.float32)
        s_own = jnp.where(own_pos[None, :] <= q_pos[:, None], s_own, NEG)
        Qc = MOBA_Q_CHUNK
        logits = jnp.concatenate([s_sel.reshape(B, H, Qc, ksel * MOBA_BLOCK), s_own], axis=-1)
        p = jax.nn.softmax(logits, axis=-1).astype(v.dtype)
        p_sel = p[..., :ksel * MOBA_BLOCK].reshape(B, H, Qc, ksel, MOBA_BLOCK)
        p_own = p[..., ksel * MOBA_BLOCK:]
        return (jnp.einsum('bhqrk,bhqrkd->bhqd', p_sel, v_sel)
                + jnp.einsum('bhqk,bhkd->bhqd', p_own, v_own))

    out = lax.map(chunk, (qc, jnp.arange(nchunk)))
    return out.transpose(1, 2, 0, 3, 4).reshape(B, H, S, dh)


def token_mixer(x, w_in, w_gate, b_gate, w_branch_sb, w_branch_moba, w_out, rel_bias):
    B, S, D = x.shape
    proj = x @ w_in
    cuts = np.cumsum([SB_WIDTH, SB_WIDTH, SB_WIDTH, MOBA_WIDTH, MOBA_WIDTH])
    q_sb, k_sb, v_sb, q_mb, k_mb, v_mb = jnp.split(proj, cuts, axis=-1)

    def heads(t, h):
        return t.reshape(B, S, h, HEAD_DIM).transpose(0, 2, 1, 3)

    def merge(t):
        return t.transpose(0, 2, 1, 3).reshape(B, S, -1)

    o_sb = stick_breaking_attention(heads(q_sb, SB_HEADS), heads(k_sb, SB_HEADS), heads(v_sb, SB_HEADS))
    o_mb = moba_attention(heads(q_mb, MOBA_HEADS), heads(k_mb, MOBA_HEADS), heads(v_mb, MOBA_HEADS), rel_bias)
    y_sb = merge(o_sb) @ w_branch_sb
    y_mb = merge(o_mb) @ w_branch_moba
    gates = jax.nn.sigmoid((x @ w_gate + b_gate).astype(jnp.float32)).astype(x.dtype)
    g_sb, g_mb = jnp.split(gates, 2, axis=-1)
    return (g_sb * y_sb + g_mb * y_mb) @ w_out


def peer_ffn(x, w_peer_query, peer_sub_keys, peer_u, peer_v):
    B, S, D = x.shape
    tokens = x.reshape(-1, D)
    nchunk = tokens.shape[0] // PEER_TOK_CHUNK
    Tc, H, K = PEER_TOK_CHUNK, PEER_HEADS, PEER_TOPK

    def chunk(xc):
        q = (xc @ w_peer_query).reshape(Tc, H, 2, PEER_QDIM // 2)
        s = jnp.einsum('thpd,pnd->thpn', q, peer_sub_keys, preferred_element_type=jnp.float32)
        top_s, top_i = lax.top_k(s, K)
        cand_s = top_s[:, :, 0, :, None] + top_s[:, :, 1, None, :]
        cand_i = top_i[:, :, 0, :, None] * PEER_NKEYS + top_i[:, :, 1, None, :]
        best_s, best_pos = lax.top_k(cand_s.reshape(Tc, H, K * K), K)
        expert = jnp.take_along_axis(cand_i.reshape(Tc, H, K * K), best_pos, axis=-1)
        g = jax.nn.softmax(best_s, axis=-1)
        u = peer_u[expert]
        v = peer_v[expert]
        act = jax.nn.gelu(jnp.einsum('td,thkd->thk', xc, u, preferred_element_type=jnp.float32),
                          approximate=False)
        return jnp.einsum('thk,thkd->td', (g * act).astype(v.dtype), v)

    out = lax.map(chunk, tokens.reshape(nchunk, Tc, D))
    return out.reshape(B, S, D)


def setup_inputs(seed: int = 0) -> dict:
    key = jax.random.key(seed)
    ks = jax.random.split(key, 20)
    D = D_MODEL
    f = jnp.float32
    sd = D ** -0.5
    x = jax.random.normal(ks[0], (BATCH, SEQ, D), f)
    w_qk_sb = jax.random.normal(ks[1], (D, 2 * SB_WIDTH), f) * sd
    w_v_sb = jax.random.normal(ks[2], (D, SB_WIDTH), f) * sd * DN_BETA
    w_qk_mb = jax.random.normal(ks[3], (D, 2 * MOBA_WIDTH), f) * sd
    w_v_mb = jax.random.normal(ks[4], (D, MOBA_WIDTH), f) * sd * DN_BETA
    w_in = jnp.concatenate([w_qk_sb, w_v_sb, w_qk_mb, w_v_mb], axis=1)
    w_gate = jax.random.normal(ks[5], (D, 2 * D), f) * sd
    b_gate = jax.random.normal(ks[6], (2 * D,), f) * 0.01
    w_branch_sb = jax.random.normal(ks[7], (SB_WIDTH, D), f) * SB_WIDTH ** -0.5 * DN_BETA
    w_branch_moba = jax.random.normal(ks[8], (MOBA_WIDTH, D), f) * MOBA_WIDTH ** -0.5 * DN_BETA
    w_out = jax.random.normal(ks[9], (D, D), f) * sd * DN_BETA
    rel_bias = jax.random.normal(ks[10], (MOBA_HEADS, REL_BUCKETS), f) * 0.2
    ln1_g = 1.0 + 0.01 * jax.random.normal(ks[11], (D,), f)
    ln1_b = 0.01 * jax.random.normal(ks[12], (D,), f)
    w_peer_query = jax.random.normal(ks[13], (D, PEER_HEADS * PEER_QDIM), f) * sd
    peer_sub_keys = jax.random.normal(ks[14], (2, PEER_NKEYS, PEER_QDIM // 2), f) * (PEER_QDIM // 2) ** -0.5
    peer_u = jax.random.normal(ks[15], (PEER_N_EXPERTS, D), f) * sd
    peer_v = jax.random.normal(ks[16], (PEER_N_EXPERTS, D), f) * (PEER_HEADS * PEER_TOPK) ** -0.5 * DN_BETA
    ln2_g = 1.0 + 0.01 * jax.random.normal(ks[17], (D,), f)
    ln2_b = 0.01 * jax.random.normal(ks[18], (D,), f)
    return {"x": x, "w_in": w_in, "w_gate": w_gate, "b_gate": b_gate,
            "w_branch_sb": w_branch_sb, "w_branch_moba": w_branch_moba, "w_out": w_out,
            "rel_bias": rel_bias, "ln1_g": ln1_g, "ln1_b": ln1_b,
            "w_peer_query": w_peer_query, "peer_sub_keys": peer_sub_keys,
            "peer_u": peer_u, "peer_v": peer_v, "ln2_g": ln2_g, "ln2_b": ln2_b}


def reference(x, w_in, w_gate, b_gate, w_branch_sb, w_branch_moba, w_out, rel_bias,
              ln1_g, ln1_b, w_peer_query, peer_sub_keys, peer_u, peer_v, ln2_g, ln2_b):
    h = x
    for _ in range(DEPTH):
        mix = token_mixer(h, w_in, w_gate, b_gate, w_branch_sb, w_branch_moba, w_out, rel_bias)
        h = layer_norm(DN_ALPHA * h + mix, ln1_g, ln1_b)
        ffn = peer_ffn(h, w_peer_query, peer_sub_keys, peer_u, peer_v)
        h = layer_norm(DN_ALPHA * h + ffn, ln2_g, ln2_b)
    return h
```

```python
import functools
import math

import numpy as np
import jax
import jax.numpy as jnp
from jax import lax
from jax.experimental import pallas as pl
from jax.experimental.pallas import tpu as pltpu

HEAD_DIM = 128
SB_HEADS = 8
MOBA_HEADS = 8
SB_BLOCK = 128
MOBA_BLOCK = 256
MOBA_TOPK = 3
REL_BUCKETS = 32
REL_MAX_DIST = 1024
BIAS_TILES = 5
PEER_HEADS = 8
PEER_NKEYS = 128
PEER_QDIM = 256
PEER_TOPK = 16
LN_EPS = 1e-5
DN_ALPHA = 2.0 ** 0.25
NEG = -1e30
LANES = 128
PEER_TOK = 8
VMEM_LIMIT = 56 * 1024 * 1024

_F32 = jnp.float32
_BF16 = jnp.bfloat16


def _nt_dot(a, b):
    return lax.dot_general(a, b, (((1,), (1,)), ((), ())), preferred_element_type=_F32)


def _dot(a, b):
    return jnp.dot(a, b, preferred_element_type=_F32)


def _params(sem):
    return pltpu.CompilerParams(dimension_semantics=sem, vmem_limit_bytes=VMEM_LIMIT)


def _proj_kernel(x_ref, w_ref, o_ref, m_ref):
    acc = _dot(x_ref[...], w_ref[...])
    o_ref[...] = acc.astype(o_ref.dtype)
    tm, tn = acc.shape
    nb = tm // MOBA_BLOCK
    m_ref[...] = jnp.mean(acc.reshape(nb, MOBA_BLOCK, tn), axis=1)[:, None, :]


def _proj(x_bf, w_bf, tm=1024, tn=512):
    t, d = x_bf.shape
    n = w_bf.shape[1]
    nb = tm // MOBA_BLOCK
    return pl.pallas_call(
        _proj_kernel,
        grid=(t // tm, n // tn),
        in_specs=[pl.BlockSpec((tm, d), lambda i, j: (i, 0)),
                  pl.BlockSpec((d, tn), lambda i, j: (0, j))],
        out_specs=[pl.BlockSpec((tm, tn), lambda i, j: (i, j)),
                   pl.BlockSpec((nb, 1, tn), lambda i, j: (i, 0, j))],
        out_shape=[jax.ShapeDtypeStruct((t, n), _BF16),
                   jax.ShapeDtypeStruct((t // MOBA_BLOCK, 1, n), _F32)],
        compiler_params=_params(("parallel", "arbitrary")),
        name="proj",
    )(x_bf, w_bf)


def _sb_kernel(q_ref, k_ref, v_ref, tri_ref, o_ref, *, scale):
    i = pl.program_id(2)
    q = q_ref[0]
    tri = tri_ref[...]
    row = lax.broadcasted_iota(jnp.int32, (SB_BLOCK, SB_BLOCK), 0)
    col = lax.broadcasted_iota(jnp.int32, (SB_BLOCK, SB_BLOCK), 1)
    causal = col < row

    def tile(j, c, acc, diag):
        start = pl.multiple_of(j * SB_BLOCK, SB_BLOCK)
        k = k_ref[0, pl.ds(start, SB_BLOCK), :]
        v = v_ref[0, pl.ds(start, SB_BLOCK), :]
        z = _nt_dot(q, k) * scale
        lg = -(jnp.maximum(z, 0.0) + jnp.log1p(jnp.exp(-jnp.abs(z))))
        if diag:
            lg = jnp.where(causal, lg, 0.0)
        lg_hi = lg.astype(_BF16)
        lg_lo = (lg - lg_hi.astype(_F32)).astype(_BF16)
        tr = _dot(lg_hi, tri) + _dot(lg_lo, tri)
        w = jnp.exp(z + lg + tr[:, :SB_BLOCK] + c)
        if diag:
            w = jnp.where(causal, w, 0.0)
        acc = acc + _dot(w.astype(_BF16), v)
        return c + tr[:, SB_BLOCK:], acc

    zero = jnp.zeros((SB_BLOCK, HEAD_DIM), _F32)
    c, acc = tile(i, zero, zero, True)

    def body(jj, carry):
        return tile(i - 1 - jj, carry[0], carry[1], False)

    c, acc = lax.fori_loop(0, i, body, (c, acc))
    o_ref[0] = acc.astype(o_ref.dtype)


def _sb_attention(proj3, tri, col0):
    b, s, _ = proj3.shape
    nq = s // SB_BLOCK
    return pl.pallas_call(
        functools.partial(_sb_kernel, scale=HEAD_DIM ** -0.5),
        grid=(b, SB_HEADS, nq),
        in_specs=[pl.BlockSpec((1, SB_BLOCK, HEAD_DIM), lambda bi, h, i: (bi, i, col0 + h)),
                  pl.BlockSpec((1, s, HEAD_DIM), lambda bi, h, i: (bi, 0, col0 + SB_HEADS + h)),
                  pl.BlockSpec((1, s, HEAD_DIM), lambda bi, h, i: (bi, 0, col0 + 2 * SB_HEADS + h)),
                  pl.BlockSpec((SB_BLOCK, 2 * SB_BLOCK), lambda bi, h, i: (0, 0))],
        out_specs=pl.BlockSpec((1, SB_BLOCK, HEAD_DIM), lambda bi, h, i: (bi, i, h)),
        out_shape=jax.ShapeDtypeStruct((b, s, SB_HEADS * HEAD_DIM), _BF16),
        compiler_params=_params(("parallel", "parallel", "arbitrary")),
        name="sb_attn",
    )(proj3, proj3, proj3, tri)


def _rel_bucket_table():
    n_exact = REL_BUCKETS // 2
    d = np.arange(BIAS_TILES)[:, None, None]
    q = np.arange(MOBA_BLOCK)[None, :, None]
    k = np.arange(MOBA_BLOCK)[None, None, :]
    rel = np.maximum(d * MOBA_BLOCK + q - k, 0)
    logd = (np.log(np.maximum(rel, 1).astype(np.float32) / np.float32(n_exact))
            / np.float32(math.log(REL_MAX_DIST / n_exact))).astype(np.float32)
    large = n_exact + (logd * np.float32(REL_BUCKETS - n_exact)).astype(np.int32)
    large = np.minimum(large, REL_BUCKETS - 1)
    return np.where(rel < n_exact, rel, large).astype(np.int32)


def _moba_kernel(rb_ref, q_ref, k_ref, v_ref, km_ref, bkt_ref, o_ref, bias_ref, *, scale):
    h = pl.program_id(0)
    first = jnp.logical_and(pl.program_id(1) == 0, pl.program_id(2) == 0)
    i = pl.program_id(2)

    @pl.when(first)
    def _():
        for d in range(BIAS_TILES):
            bkt = bkt_ref[d]
            tile = jnp.zeros((MOBA_BLOCK, MOBA_BLOCK), _F32)
            for bi in range(REL_BUCKETS):
                tile = jnp.where(bkt == bi, rb_ref[h, bi], tile)
            bias_ref[d] = tile
        bias_ref[BIAS_TILES] = jnp.full((MOBA_BLOCK, MOBA_BLOCK), rb_ref[h, REL_BUCKETS - 1], _F32)

    q = q_ref[0]
    lane = lax.broadcasted_iota(jnp.int32, (MOBA_BLOCK, LANES), 1)
    valid = lane < i
    gate = _nt_dot(q, km_ref[0].astype(_BF16))
    g = jnp.where(valid, gate, -jnp.inf)
    sel = jnp.zeros((MOBA_BLOCK, LANES), jnp.bool_)
    for _ in range(MOBA_TOPK):
        m = jnp.max(g, axis=-1, keepdims=True)
        first_idx = jnp.min(jnp.where(g == m, lane, LANES), axis=-1, keepdims=True)
        pick = lane == first_idx
        sel = jnp.logical_or(sel, pick)
        g = jnp.where(pick, -jnp.inf, g)
    sel_f = jnp.where(jnp.logical_and(sel, valid), 1.0, 0.0)

    row = lax.broadcasted_iota(jnp.int32, (MOBA_BLOCK, MOBA_BLOCK), 0)
    col = lax.broadcasted_iota(jnp.int32, (MOBA_BLOCK, MOBA_BLOCK), 1)

    def block(j):
        start = pl.multiple_of(j * MOBA_BLOCK, MOBA_BLOCK)
        return k_ref[0, pl.ds(start, MOBA_BLOCK), :], v_ref[0, pl.ds(start, MOBA_BLOCK), :]

    kb, vb = block(i)
    s = _nt_dot(q, kb) * scale + bias_ref[0]
    s = jnp.where(col <= row, s, NEG)
    m0 = jnp.max(s, axis=-1, keepdims=True)
    p = jnp.exp(s - m0)
    l0 = jnp.sum(p, axis=-1, keepdims=True)
    acc0 = _dot(p.astype(_BF16), vb)

    def body(j, carry):
        m, l, acc = carry
        kb, vb = block(j)
        d = jnp.minimum(i - j, BIAS_TILES)
        s = _nt_dot(q, kb) * scale + bias_ref[d]
        chosen = jnp.sum(jnp.where(lane == j, sel_f, 0.0), axis=-1, keepdims=True)
        s = jnp.where(chosen > 0.5, s, NEG)
        m_new = jnp.maximum(m, jnp.max(s, axis=-1, keepdims=True))
        a = jnp.exp(m - m_new)
        p = jnp.exp(s - m_new)
        l = a * l + jnp.sum(p, axis=-1, keepdims=True)
        acc = a * acc + _dot(p.astype(_BF16), vb)
        return m_new, l, acc

    m, l, acc = lax.fori_loop(0, i, body, (m0, l0, acc0))
    o_ref[0] = (acc / l).astype(o_ref.dtype)


def _moba_attention(proj3, kmean_pad, rel_bias, col0):
    b, s, _ = proj3.shape
    nq = s // MOBA_BLOCK
    bkt = jnp.asarray(_rel_bucket_table())
    return pl.pallas_call(
        functools.partial(_moba_kernel, scale=HEAD_DIM ** -0.5),
        grid=(MOBA_HEADS, b, nq),
        in_specs=[pl.BlockSpec(memory_space=pltpu.SMEM),
                  pl.BlockSpec((1, MOBA_BLOCK, HEAD_DIM), lambda h, bi, i: (bi, i, col0 + h)),
                  pl.BlockSpec((1, s, HEAD_DIM), lambda h, bi, i: (bi, 0, col0 + MOBA_HEADS + h)),
                  pl.BlockSpec((1, s, HEAD_DIM), lambda h, bi, i: (bi, 0, col0 + 2 * MOBA_HEADS + h)),
                  pl.BlockSpec((1, LANES, HEAD_DIM), lambda h, bi, i: (bi, 0, h)),
                  pl.BlockSpec((BIAS_TILES, MOBA_BLOCK, MOBA_BLOCK), lambda h, bi, i: (0, 0, 0))],
        out_specs=pl.BlockSpec((1, MOBA_BLOCK, HEAD_DIM), lambda h, bi, i: (bi, i, h)),
        out_shape=jax.ShapeDtypeStruct((b, s, MOBA_HEADS * HEAD_DIM), _BF16),
        scratch_shapes=[pltpu.VMEM((BIAS_TILES + 1, MOBA_BLOCK, MOBA_BLOCK), _F32)],
        compiler_params=_params(("arbitrary", "arbitrary", "arbitrary")),
        name="moba_attn",
    )(rel_bias, proj3, proj3, proj3, kmean_pad, bkt)


def _mix_kernel(x_ref, osb_ref, omb_ref, wg1_ref, wg2_ref, b1_ref, b2_ref, wsb_ref, wmb_ref, o_ref):
    x = x_ref[...]
    g_sb = jax.nn.sigmoid(_dot(x, wg1_ref[...]) + b1_ref[...])
    g_mb = jax.nn.sigmoid(_dot(x, wg2_ref[...]) + b2_ref[...])
    y_sb = _dot(osb_ref[...], wsb_ref[...])
    y_mb = _dot(omb_ref[...], wmb_ref[...])
    o_ref[...] = (g_sb * y_sb + g_mb * y_mb).astype(o_ref.dtype)


def _mix(x_bf, o_sb, o_mb, wg_bf, b_gate, wsb_bf, wmb_bf, tm=1024, tn=512):
    t, d = x_bf.shape
    w = o_sb.shape[1]
    nj = d // tn
    b2d = b_gate.reshape(1, 2 * d)
    return pl.pallas_call(
        _mix_kernel,
        grid=(t // tm, nj),
        in_specs=[pl.BlockSpec((tm, d), lambda i, j: (i, 0)),
                  pl.BlockSpec((tm, w), lambda i, j: (i, 0)),
                  pl.BlockSpec((tm, w), lambda i, j: (i, 0)),
                  pl.BlockSpec((d, tn), lambda i, j: (0, j)),
                  pl.BlockSpec((d, tn), lambda i, j: (0, nj + j)),
                  pl.BlockSpec((1, tn), lambda i, j: (0, j)),
                  pl.BlockSpec((1, tn), lambda i, j: (0, nj + j)),
                  pl.BlockSpec((w, tn), lambda i, j: (0, j)),
                  pl.BlockSpec((w, tn), lambda i, j: (0, j))],
        out_specs=pl.BlockSpec((tm, tn), lambda i, j: (i, j)),
        out_shape=jax.ShapeDtypeStruct((t, d), _BF16),
        compiler_params=_params(("parallel", "arbitrary")),
        name="mix",
    )(x_bf, o_sb, o_mb, wg_bf, wg_bf, b2d, b2d, wsb_bf, wmb_bf)


def _layer_norm(h, g, b):
    mu = jnp.mean(h, axis=-1, keepdims=True)
    xc = h - mu
    var = jnp.mean(xc * xc, axis=-1, keepdims=True)
    return xc * lax.rsqrt(var + LN_EPS) * g + b


def _outln_kernel(m_ref, w_ref, x_ref, g_ref, b_ref, o_ref, obf_ref):
    h = DN_ALPHA * x_ref[...] + _dot(m_ref[...], w_ref[...])
    y = _layer_norm(h, g_ref[...], b_ref[...])
    o_ref[...] = y
    obf_ref[...] = y.astype(_BF16)


def _outln(mix_bf, w_bf, x2, g, b, tm=256):
    t, d = x2.shape
    row = pl.BlockSpec((tm, d), lambda i: (i, 0))
    vec = pl.BlockSpec((1, d), lambda i: (0, 0))
    return pl.pallas_call(
        _outln_kernel,
        grid=(t // tm,),
        in_specs=[row, pl.BlockSpec((d, d), lambda i: (0, 0)), row, vec, vec],
        out_specs=[row, row],
        out_shape=[jax.ShapeDtypeStruct((t, d), _F32), jax.ShapeDtypeStruct((t, d), _BF16)],
        compiler_params=_params(("parallel",)),
        name="outln",
    )(mix_bf, w_bf, x2, g.reshape(1, d), b.reshape(1, d))


def _pscore_kernel(x_ref, w_ref, sk_ref, o_ref):
    pq = _dot(x_ref[...], w_ref[...]).astype(_BF16)
    half = PEER_QDIM // 2
    s0 = _nt_dot(pq[:, :half], sk_ref[0].astype(_BF16))
    s1 = _nt_dot(pq[:, half:], sk_ref[1].astype(_BF16))
    o_ref[...] = jnp.concatenate([s0, s1], axis=-1)


def _pscore(x1_bf, wq_bf, sub_keys, tm=1024):
    t, d = x1_bf.shape
    return pl.pallas_call(
        _pscore_kernel,
        grid=(t // tm, PEER_HEADS),
        in_specs=[pl.BlockSpec((tm, d), lambda i, h: (i, 0)),
                  pl.BlockSpec((d, PEER_QDIM), lambda i, h: (0, h)),
                  pl.BlockSpec((2, PEER_NKEYS, PEER_QDIM // 2), lambda i, h: (0, 0, 0))],
        out_specs=pl.BlockSpec((tm, 2 * PEER_NKEYS), lambda i, h: (i, h)),
        out_shape=jax.ShapeDtypeStruct((t, PEER_HEADS * 2 * PEER_NKEYS), _F32),
        compiler_params=_params(("parallel", "arbitrary")),
        name="pscore",
    )(x1_bf, wq_bf, sub_keys)


def _ptopk_kernel(s_ref, idx_ref, g_ref):
    h = pl.program_id(1)
    tm = s_ref.shape[0]
    k = PEER_TOPK
    lane = lax.broadcasted_iota(jnp.int32, (tm, LANES), 1).astype(_F32)
    lane2 = lax.broadcasted_iota(jnp.int32, (tm, 2 * LANES), 1)
    pos2 = lane2.astype(_F32)
    slot_hi = lane2 // k
    slot_lo = lane2 % k

    cand = jnp.zeros((tm, 2 * LANES), _F32)
    expert = jnp.zeros((tm, 2 * LANES), _F32)
    for p in range(2):
        s = s_ref[:, p * PEER_NKEYS:(p + 1) * PEER_NKEYS]
        slot = slot_hi if p == 0 else slot_lo
        val = jnp.zeros((tm, 2 * LANES), _F32)
        key = jnp.zeros((tm, 2 * LANES), _F32)
        for r in range(k):
            m = jnp.max(s, axis=-1, keepdims=True)
            first_idx = jnp.min(jnp.where(s == m, lane, float(LANES)), axis=-1, keepdims=True)
            val = jnp.where(slot == r, m, val)
            key = jnp.where(slot == r, first_idx, key)
            s = jnp.where(lane == first_idx, -jnp.inf, s)
        cand = cand + val
        expert = expert + key * (float(PEER_NKEYS) if p == 0 else 1.0)

    @pl.when(h == 0)
    def _():
        idx_ref[...] = jnp.zeros_like(idx_ref)
        g_ref[...] = jnp.zeros_like(g_ref)

    out_lane = lax.broadcasted_iota(jnp.int32, (tm, LANES), 1)
    ids = idx_ref[...]
    ex = g_ref[...]
    den = None
    m_top = None
    for r in range(k):
        m = jnp.max(cand, axis=-1, keepdims=True)
        first_pos = jnp.min(jnp.where(cand == m, pos2, float(2 * LANES)), axis=-1, keepdims=True)
        hit = pos2 == first_pos
        e = jnp.max(jnp.where(hit, expert, -1.0), axis=-1, keepdims=True)
        if r == 0:
            m_top = m
        w = jnp.exp(m - m_top)
        den = w if r == 0 else den + w
        here = out_lane == h * k + r
        ids = jnp.where(here, e.astype(jnp.int32), ids)
        ex = jnp.where(here, w, ex)
        cand = jnp.where(hit, -jnp.inf, cand)
    head_lanes = (out_lane // k) == h
    idx_ref[...] = ids
    g_ref[...] = jnp.where(head_lanes, ex / den, ex)


def _ptopk(scores, tm=128):
    t = scores.shape[0]
    out = pl.BlockSpec((tm, LANES), lambda i, h: (i, 0))
    return pl.pallas_call(
        _ptopk_kernel,
        grid=(t // tm, PEER_HEADS),
        in_specs=[pl.BlockSpec((tm, 2 * PEER_NKEYS), lambda i, h: (i, h))],
        out_specs=[out, out],
        out_shape=[jax.ShapeDtypeStruct((t, LANES), jnp.int32), jax.ShapeDtypeStruct((t, LANES), _F32)],
        compiler_params=_params(("parallel", "arbitrary")),
        name="ptopk",
    )(scores)


def _peer_kernel(idx_ref, g_ref, x_ref, uv_hbm, grp_ref, grpt_ref, diag_ref, lg_ref, lb_ref,
                 o_ref, buf, arow, sem):
    nexp = PEER_HEADS * PEER_TOPK
    sub = uv_hbm.shape[1] // 2

    def row_copy(t, e, prio):
        src = uv_hbm.at[idx_ref[t * nexp + e]]
        return pltpu.make_async_copy(src, buf.at[t * nexp + e], sem.at[t])

    def issue(t, _):
        for e in range(nexp):
            row_copy(t, e, e % 2).start(priority=e % 2)
        return 0

    lax.fori_loop(0, PEER_TOK, issue, 0)

    diag = diag_ref[...]
    for t in range(PEER_TOK):
        pltpu.make_async_copy(uv_hbm.at[pl.ds(0, nexp)], buf.at[pl.ds(t * nexp, nexp)], sem.at[t]).wait()
        u = buf[pl.ds(t * nexp, nexp), 0:sub, :].reshape(nexp * sub, LANES)
        part = _nt_dot(x_ref[t].astype(_BF16), u) * diag
        arow[pl.ds(t, 1), :] = jnp.sum(part, axis=0, keepdims=True)

    a = arow[...]
    a_hi = a.astype(_BF16)
    a_lo = (a - a_hi.astype(_F32)).astype(_BF16)
    act = _dot(a_hi, grp_ref[...]) + _dot(a_lo, grp_ref[...])
    gelu = 0.5 * act * (1.0 + lax.erf(act * (2.0 ** -0.5)))
    w = (g_ref[...] * gelu).astype(_BF16)
    w_rep = _dot(w, grpt_ref[...])

    for t in range(PEER_TOK):
        v = buf[pl.ds(t * nexp, nexp), sub:2 * sub, :].reshape(nexp * sub, LANES)
        lhs = (w_rep[t:t + 1, :] * diag).astype(_BF16)
        h = DN_ALPHA * x_ref[t] + _dot(lhs, v)
        mu = jnp.mean(h)
        xc = h - mu
        var = jnp.mean(xc * xc)
        o_ref[t] = xc * lax.rsqrt(var + LN_EPS) * lg_ref[...] + lb_ref[...]


def _peer(idx_flat, g, x1r, uv, ln_g, ln_b):
    t, sub, _ = x1r.shape
    nexp = PEER_HEADS * PEER_TOPK
    d = sub * LANES
    pos = np.arange(nexp * sub)
    grp = (pos[:, None] // sub == np.arange(nexp)[None, :])
    diag = (np.arange(sub)[:, None] == pos[None, :] % sub)
    grp_bf = jnp.asarray(grp, _BF16)
    const2 = lambda shape: pl.BlockSpec(shape, lambda i: (0, 0))
    tok3 = pl.BlockSpec((PEER_TOK, sub, LANES), lambda i: (i, 0, 0))
    return pl.pallas_call(
        _peer_kernel,
        grid=(t // PEER_TOK,),
        in_specs=[pl.BlockSpec((PEER_TOK * nexp,), lambda i: (i,), memory_space=pltpu.SMEM),
                  pl.BlockSpec((PEER_TOK, nexp), lambda i: (i, 0)),
                  tok3,
                  pl.BlockSpec(memory_space=pl.ANY),
                  const2((nexp * sub, nexp)),
                  const2((nexp, nexp * sub)),
                  const2((sub, nexp * sub)),
                  const2((sub, LANES)),
                  const2((sub, LANES))],
        out_specs=tok3,
        out_shape=jax.ShapeDtypeStruct((t, sub, LANES), _F32),
        scratch_shapes=[pltpu.VMEM((PEER_TOK * nexp, 2 * sub, LANES), _BF16),
                        pltpu.VMEM((PEER_TOK, nexp * sub), _F32),
                        pltpu.SemaphoreType.DMA((PEER_TOK,))],
        compiler_params=_params(("arbitrary",)),
        name="peer",
    )(idx_flat, g, x1r, uv, grp_bf, grp_bf.T, jnp.asarray(diag, _F32),
      ln_g.reshape(sub, LANES), ln_b.reshape(sub, LANES))


def kernel(x, w_in, w_gate, b_gate, w_branch_sb, w_branch_moba, w_out, rel_bias, ln1_g, ln1_b,
           w_peer_query, peer_sub_keys, peer_u, peer_v, ln2_g, ln2_b):
    b, s, d = x.shape
    t = b * s
    x2 = x.reshape(t, d)
    x_bf = x2.astype(_BF16)
    sb_w = SB_HEADS * HEAD_DIM
    mb_w = MOBA_HEADS * HEAD_DIM

    proj, colmean = _proj(x_bf, w_in.astype(_BF16))
    proj3 = proj.reshape(b, s, proj.shape[1])
    nblk = s // MOBA_BLOCK
    kmean = colmean.reshape(b, nblk, -1)[:, :, 3 * sb_w + mb_w:3 * sb_w + 2 * mb_w]
    kmean_pad = jnp.pad(kmean, ((0, 0), (0, LANES - nblk), (0, 0)))

    row = np.arange(SB_BLOCK)[:, None]
    col = np.arange(2 * SB_BLOCK)[None, :]
    tri = jnp.asarray((row > col) | (col >= SB_BLOCK), _BF16)

    o_sb = _sb_attention(proj3, tri, 0)
    o_mb = _moba_attention(proj3, kmean_pad, rel_bias, 3 * sb_w // HEAD_DIM)
    mix = _mix(x_bf, o_sb.reshape(t, sb_w), o_mb.reshape(t, mb_w), w_gate.astype(_BF16), b_gate,
               w_branch_sb.astype(_BF16), w_branch_moba.astype(_BF16))
    x1, x1_bf = _outln(mix, w_out.astype(_BF16), x2, ln1_g, ln1_b)

    scores = _pscore(x1_bf, w_peer_query.astype(_BF16), peer_sub_keys)
    idx, g = _ptopk(scores)
    sub = d // LANES
    n_exp = peer_u.shape[0]
    uv = jnp.concatenate([peer_u.astype(_BF16).reshape(n_exp, sub, LANES),
                          peer_v.astype(_BF16).reshape(n_exp, sub, LANES)], axis=1)
    out = _peer(idx.reshape(-1), g, x1.reshape(t, sub, LANES), uv, ln2_g, ln2_b)
    return out.reshape(b, s, d)
```

```python
import functools
import math

import numpy as np
import jax
import jax.numpy as jnp
from jax import lax
from jax.experimental import pallas as pl
from jax.experimental.pallas import tpu as pltpu

HEAD_DIM = 128
SB_HEADS = 8
MOBA_HEADS = 8
SB_BLOCK = 128
SB_TQ = 512
SB_GROUP = 1
SB_DEAD_LOG = -110.0
MOBA_BLOCK = 256
MOBA_TOPK = 3
REL_BUCKETS = 32
REL_MAX_DIST = 1024
BIAS_TILES = 5
PEER_HEADS = 8
PEER_NKEYS = 128
PEER_QDIM = 256
PEER_TOPK = 16
LN_EPS = 1e-5
DN_ALPHA = 2.0 ** 0.25
NEG = -1e30
LANES = 128
PEER_TOK = 8
VMEM_LIMIT = 56 * 1024 * 1024

_F32 = jnp.float32
_BF16 = jnp.bfloat16


def _nt_dot(a, b):
    return lax.dot_general(a, b, (((1,), (1,)), ((), ())), preferred_element_type=_F32)


def _dot(a, b):
    return jnp.dot(a, b, preferred_element_type=_F32)


def _params(sem):
    return pltpu.CompilerParams(dimension_semantics=sem, vmem_limit_bytes=VMEM_LIMIT)


def _proj_kernel(x_ref, w_ref, o_ref, m_ref):
    acc = _dot(x_ref[...], w_ref[...])
    o_ref[...] = acc.astype(o_ref.dtype)
    tm, tn = acc.shape
    nb = tm // MOBA_BLOCK
    m_ref[...] = jnp.mean(acc.reshape(nb, MOBA_BLOCK, tn), axis=1)[:, None, :]


def _proj(x_bf, w_bf, tm=1024, tn=512):
    t, d = x_bf.shape
    n = w_bf.shape[1]
    nb = tm // MOBA_BLOCK
    return pl.pallas_call(
        _proj_kernel,
        grid=(t // tm, n // tn),
        in_specs=[pl.BlockSpec((tm, d), lambda i, j: (i, 0)),
                  pl.BlockSpec((d, tn), lambda i, j: (0, j))],
        out_specs=[pl.BlockSpec((tm, tn), lambda i, j: (i, j)),
                   pl.BlockSpec((nb, 1, tn), lambda i, j: (i, 0, j))],
        out_shape=[jax.ShapeDtypeStruct((t, n), _BF16),
                   jax.ShapeDtypeStruct((t // MOBA_BLOCK, 1, n), _F32)],
        compiler_params=_params(("parallel", "arbitrary")),
        name="proj",
    )(x_bf, w_bf)


def _sb_kernel(q_ref, k_ref, v_ref, tri_ref, o_ref, c_ref, acc_ref, *, scale):
    i = pl.program_id(2)
    nsub = SB_TQ // SB_BLOCK
    tri = tri_ref[...]
    row = lax.broadcasted_iota(jnp.int32, (SB_TQ, SB_BLOCK), 0)
    col = lax.broadcasted_iota(jnp.int32, (SB_TQ, SB_BLOCK), 1)

    def tile(g, j, causal):
        lanes = slice(g * HEAD_DIM, (g + 1) * HEAD_DIM)
        start = pl.multiple_of(j * SB_BLOCK, SB_BLOCK)
        k = k_ref[0, pl.ds(start, SB_BLOCK), lanes]
        v = v_ref[0, pl.ds(start, SB_BLOCK), lanes]
        z = _nt_dot(q_ref[0, :, lanes], k) * scale
        lg = -(jnp.maximum(z, 0.0) + jnp.log1p(jnp.exp(-jnp.abs(z))))
        if causal is not None:
            lg = jnp.where(causal, lg, 0.0)
        lg_hi = lg.astype(_BF16)
        lg_lo = (lg - lg_hi.astype(_F32)).astype(_BF16)
        tr = _dot(lg_hi, tri) + _dot(lg_lo, tri)
        w = jnp.exp(z + lg + tr[:, :SB_BLOCK] + c_ref[g])
        if causal is not None:
            w = jnp.where(causal, w, 0.0)
        c_ref[g] += tr[:, SB_BLOCK:]
        acc_ref[g] += _dot(w.astype(_BF16), v)

    c_ref[...] = jnp.zeros_like(c_ref)
    acc_ref[...] = jnp.zeros_like(acc_ref)

    def diag_body(jj, carry):
        dj = nsub - 1 - jj
        causal = col + dj * SB_BLOCK < row
        for g in range(SB_GROUP):
            tile(g, nsub * i + dj, causal)
        return carry

    def live(carry):
        jj, log_rest = carry
        return jnp.logical_and(jj < nsub * i, log_rest > SB_DEAD_LOG)

    def body(carry):
        jj, _ = carry
        for g in range(SB_GROUP):
            tile(g, nsub * i - 1 - jj, None)
        return jj + 1, jnp.max(c_ref[...])

    lax.fori_loop(0, nsub, diag_body, 0)
    lax.while_loop(live, body, (jnp.int32(0), jnp.max(c_ref[...])))
    for g in range(SB_GROUP):
        o_ref[0, :, g * HEAD_DIM:(g + 1) * HEAD_DIM] = acc_ref[g].astype(o_ref.dtype)


def _sb_attention(proj3, tri, col0):
    b, s, _ = proj3.shape
    nq = s // SB_TQ
    gw = SB_GROUP * HEAD_DIM
    ng = SB_HEADS // SB_GROUP
    c0 = col0 // SB_GROUP
    return pl.pallas_call(
        functools.partial(_sb_kernel, scale=HEAD_DIM ** -0.5),
        grid=(b, ng, nq),
        in_specs=[pl.BlockSpec((1, SB_TQ, gw), lambda bi, h, i: (bi, i, c0 + h)),
                  pl.BlockSpec((1, s, gw), lambda bi, h, i: (bi, 0, c0 + ng + h)),
                  pl.BlockSpec((1, s, gw), lambda bi, h, i: (bi, 0, c0 + 2 * ng + h)),
                  pl.BlockSpec((SB_BLOCK, 2 * SB_BLOCK), lambda bi, h, i: (0, 0))],
        out_specs=pl.BlockSpec((1, SB_TQ, gw), lambda bi, h, i: (bi, i, h)),
        out_shape=jax.ShapeDtypeStruct((b, s, SB_HEADS * HEAD_DIM), _BF16),
        scratch_shapes=[pltpu.VMEM((SB_GROUP, SB_TQ, HEAD_DIM), _F32),
                        pltpu.VMEM((SB_GROUP, SB_TQ, HEAD_DIM), _F32)],
        compiler_params=_params(("parallel", "parallel", "arbitrary")),
        name="sb_attn",
    )(proj3, proj3, proj3, tri)


def _rel_bucket_table():
    n_exact = REL_BUCKETS // 2
    d = np.arange(BIAS_TILES)[:, None, None]
    q = np.arange(MOBA_BLOCK)[None, :, None]
    k = np.arange(MOBA_BLOCK)[None, None, :]
    rel = np.maximum(d * MOBA_BLOCK + q - k, 0)
    logd = (np.log(np.maximum(rel, 1).astype(np.float32) / np.float32(n_exact))
            / np.float32(math.log(REL_MAX_DIST / n_exact))).astype(np.float32)
    large = n_exact + (logd * np.float32(REL_BUCKETS - n_exact)).astype(np.int32)
    large = np.minimum(large, REL_BUCKETS - 1)
    return np.where(rel < n_exact, rel, large).astype(np.int32)


def _moba_kernel(rb_ref, q_ref, k_ref, v_ref, km_ref, bkt_ref, o_ref, bias_ref, *, scale):
    h = pl.program_id(0)
    first = jnp.logical_and(pl.program_id(1) == 0, pl.program_id(2) == 0)
    i = pl.program_id(2)

    @pl.when(first)
    def _():
        for d in range(BIAS_TILES):
            bkt = bkt_ref[d]
            tile = jnp.zeros((MOBA_BLOCK, MOBA_BLOCK), _F32)
            for bi in range(REL_BUCKETS):
                tile = jnp.where(bkt == bi, rb_ref[h, bi], tile)
            bias_ref[d] = tile
        bias_ref[BIAS_TILES] = jnp.full((MOBA_BLOCK, MOBA_BLOCK), rb_ref[h, REL_BUCKETS - 1], _F32)

    q = q_ref[0]
    lane = lax.broadcasted_iota(jnp.int32, (MOBA_BLOCK, LANES), 1)
    valid = lane < i
    gate = _nt_dot(q, km_ref[0].astype(_BF16))
    g = jnp.where(valid, gate, -jnp.inf)
    sel = jnp.zeros((MOBA_BLOCK, LANES), jnp.bool_)
    for _ in range(MOBA_TOPK):
        m = jnp.max(g, axis=-1, keepdims=True)
        first_idx = jnp.min(jnp.where(g == m, lane, LANES), axis=-1, keepdims=True)
        pick = lane == first_idx
        sel = jnp.logical_or(sel, pick)
        g = jnp.where(pick, -jnp.inf, g)
    sel_f = jnp.where(jnp.logical_and(sel, valid), 1.0, 0.0)

    row = lax.broadcasted_iota(jnp.int32, (MOBA_BLOCK, MOBA_BLOCK), 0)
    col = lax.broadcasted_iota(jnp.int32, (MOBA_BLOCK, MOBA_BLOCK), 1)

    def block(j):
        start = pl.multiple_of(j * MOBA_BLOCK, MOBA_BLOCK)
        return k_ref[0, pl.ds(start, MOBA_BLOCK), :], v_ref[0, pl.ds(start, MOBA_BLOCK), :]

    kb, vb = block(i)
    s = _nt_dot(q, kb) * scale + bias_ref[0]
    s = jnp.where(col <= row, s, NEG)
    m0 = jnp.max(s, axis=-1, keepdims=True)
    p = jnp.exp(s - m0)
    l0 = jnp.sum(p, axis=-1, keepdims=True)
    acc0 = _dot(p.astype(_BF16), vb)

    def body(j, carry):
        m, l, acc = carry
        kb, vb = block(j)
        d = jnp.minimum(i - j, BIAS_TILES)
        s = _nt_dot(q, kb) * scale + bias_ref[d]
        chosen = jnp.sum(jnp.where(lane == j, sel_f, 0.0), axis=-1, keepdims=True)
        s = jnp.where(chosen > 0.5, s, NEG)
        m_new = jnp.maximum(m, jnp.max(s, axis=-1, keepdims=True))
        a = jnp.exp(m - m_new)
        p = jnp.exp(s - m_new)
        l = a * l + jnp.sum(p, axis=-1, keepdims=True)
        acc = a * acc + _dot(p.astype(_BF16), vb)
        return m_new, l, acc

    m, l, acc = lax.fori_loop(0, i, body, (m0, l0, acc0))
    o_ref[0] = (acc / l).astype(o_ref.dtype)


def _moba_attention(proj3, kmean_pad, rel_bias, col0):
    b, s, _ = proj3.shape
    nq = s // MOBA_BLOCK
    bkt = jnp.asarray(_rel_bucket_table())
    return pl.pallas_call(
        functools.partial(_moba_kernel, scale=HEAD_DIM ** -0.5),
        grid=(MOBA_HEADS, b, nq),
        in_specs=[pl.BlockSpec(memory_space=pltpu.SMEM),
                  pl.BlockSpec((1, MOBA_BLOCK, HEAD_DIM), lambda h, bi, i: (bi, i, col0 + h)),
                  pl.BlockSpec((1, s, HEAD_DIM), lambda h, bi, i: (bi, 0, col0 + MOBA_HEADS + h)),
                  pl.BlockSpec((1, s, HEAD_DIM), lambda h, bi, i: (bi, 0, col0 + 2 * MOBA_HEADS + h)),
                  pl.BlockSpec((1, LANES, HEAD_DIM), lambda h, bi, i: (bi, 0, h)),
                  pl.BlockSpec((BIAS_TILES, MOBA_BLOCK, MOBA_BLOCK), lambda h, bi, i: (0, 0, 0))],
        out_specs=pl.BlockSpec((1, MOBA_BLOCK, HEAD_DIM), lambda h, bi, i: (bi, i, h)),
        out_shape=jax.ShapeDtypeStruct((b, s, MOBA_HEADS * HEAD_DIM), _BF16),
        scratch_shapes=[pltpu.VMEM((BIAS_TILES + 1, MOBA_BLOCK, MOBA_BLOCK), _F32)],
        compiler_params=_params(("arbitrary", "arbitrary", "arbitrary")),
        name="moba_attn",
    )(rel_bias, proj3, proj3, proj3, kmean_pad, bkt)


def _mix_kernel(x_ref, osb_ref, omb_ref, wg1_ref, wg2_ref, b1_ref, b2_ref, wsb_ref, wmb_ref, o_ref):
    x = x_ref[...]
    g_sb = jax.nn.sigmoid(_dot(x, wg1_ref[...]) + b1_ref[...])
    g_mb = jax.nn.sigmoid(_dot(x, wg2_ref[...]) + b2_ref[...])
    y_sb = _dot(osb_ref[...], wsb_ref[...])
    y_mb = _dot(omb_ref[...], wmb_ref[...])
    o_ref[...] = (g_sb * y_sb + g_mb * y_mb).astype(o_ref.dtype)


def _mix(x_bf, o_sb, o_mb, wg_bf, b_gate, wsb_bf, wmb_bf, tm=1024, tn=512):
    t, d = x_bf.shape
    w = o_sb.shape[1]
    nj = d // tn
    b2d = b_gate.reshape(1, 2 * d)
    return pl.pallas_call(
        _mix_kernel,
        grid=(t // tm, nj),
        in_specs=[pl.BlockSpec((tm, d), lambda i, j: (i, 0)),
                  pl.BlockSpec((tm, w), lambda i, j: (i, 0)),
                  pl.BlockSpec((tm, w), lambda i, j: (i, 0)),
                  pl.BlockSpec((d, tn), lambda i, j: (0, j)),
                  pl.BlockSpec((d, tn), lambda i, j: (0, nj + j)),
                  pl.BlockSpec((1, tn), lambda i, j: (0, j)),
                  pl.BlockSpec((1, tn), lambda i, j: (0, nj + j)),
                  pl.BlockSpec((w, tn), lambda i, j: (0, j)),
                  pl.BlockSpec((w, tn), lambda i, j: (0, j))],
        out_specs=pl.BlockSpec((tm, tn), lambda i, j: (i, j)),
        out_shape=jax.ShapeDtypeStruct((t, d), _BF16),
        compiler_params=_params(("parallel", "arbitrary")),
        name="mix",
    )(x_bf, o_sb, o_mb, wg_bf, wg_bf, b2d, b2d, wsb_bf, wmb_bf)


def _layer_norm(h, g, b):
    mu = jnp.mean(h, axis=-1, keepdims=True)
    xc = h - mu
    var = jnp.mean(xc * xc, axis=-1, keepdims=True)
    return xc * lax.rsqrt(var + LN_EPS) * g + b


def _outln_kernel(m_ref, w_ref, x_ref, g_ref, b_ref, o_ref, obf_ref):
    h = DN_ALPHA * x_ref[...] + _dot(m_ref[...], w_ref[...])
    y = _layer_norm(h, g_ref[...], b_ref[...])
    o_ref[...] = y
    obf_ref[...] = y.astype(_BF16)


def _outln(mix_bf, w_bf, x2, g, b, tm=256):
    t, d = x2.shape
    row = pl.BlockSpec((tm, d), lambda i: (i, 0))
    vec = pl.BlockSpec((1, d), lambda i: (0, 0))
    return pl.pallas_call(
        _outln_kernel,
        grid=(t // tm,),
        in_specs=[row, pl.BlockSpec((d, d), lambda i: (0, 0)), row, vec, vec],
        out_specs=[row, row],
        out_shape=[jax.ShapeDtypeStruct((t, d), _F32), jax.ShapeDtypeStruct((t, d), _BF16)],
        compiler_params=_params(("parallel",)),
        name="outln",
    )(mix_bf, w_bf, x2, g.reshape(1, d), b.reshape(1, d))


def _pkeys_kernel(x_ref, w_ref, sk_ref, idx_ref, g_ref):
    tm = x_ref.shape[0]
    k = PEER_TOPK
    nk = PEER_NKEYS
    half = PEER_QDIM // 2
    inf = jnp.inf
    pq = _dot(x_ref[...], w_ref[...]).astype(_BF16)
    key_id = lax.broadcasted_iota(jnp.int32, (nk, tm), 0).astype(_F32)
    rank = lax.broadcasted_iota(jnp.int32, (k, tm), 0)

    def first_max(pieces, ids):
        m = pieces[0]
        for piece in pieces[1:]:
            m = jnp.maximum(m, piece)
        m = jnp.max(m, axis=0, keepdims=True)
        f = None
        for piece, pid in zip(pieces, ids):
            c = jnp.where(piece == m, pid, 1e9)
            f = c if f is None else jnp.minimum(f, c)
        return m, jnp.min(f, axis=0, keepdims=True)

    top_val, top_key = [], []
    for p in range(2):
        s = _nt_dot(sk_ref[p].astype(_BF16), pq[:, p * half:(p + 1) * half])
        val = jnp.zeros((k, tm), _F32)
        key = jnp.zeros((k, tm), _F32)
        for r in range(k):
            m, f = first_max([s], [key_id])
            val = jnp.where(rank == r, m, val)
            key = jnp.where(rank == r, f, key)
            s = jnp.where(key_id == f, -inf, s)
        top_val.append(val)
        top_key.append(key)

    rank_f = rank.astype(_F32)
    cand = [top_val[0][a:a + 1] + top_val[1] for a in range(k)]
    expert = [top_key[0][a:a + 1] * float(nk) + top_key[1] for a in range(k)]
    flat = [rank_f + float(a * k) for a in range(k)]
    best = jnp.zeros((k, tm), _F32)
    ids = jnp.zeros((k, tm), _F32)
    for r in range(k):
        m, f = first_max(cand, flat)
        e = None
        for a in range(k):
            hit = flat[a] == f
            ea = jnp.where(hit, expert[a], -1.0)
            e = ea if e is None else jnp.maximum(e, ea)
            cand[a] = jnp.where(hit, -inf, cand[a])
        best = jnp.where(rank == r, m, best)
        ids = jnp.where(rank == r, jnp.max(e, axis=0, keepdims=True), ids)
    w = jnp.exp(best - best[0:1])
    idx_ref[...] = ids.astype(jnp.int32)
    g_ref[...] = w / jnp.sum(w, axis=0, keepdims=True)


def _pkeys(x1_bf, wq_bf, sub_keys, tm=256):
    t, d = x1_bf.shape
    out = pl.BlockSpec((PEER_TOPK, tm), lambda i, h: (h, i))
    return pl.pallas_call(
        _pkeys_kernel,
        grid=(t // tm, PEER_HEADS),
        in_specs=[pl.BlockSpec((tm, d), lambda i, h: (i, 0)),
                  pl.BlockSpec((d, PEER_QDIM), lambda i, h: (0, h)),
                  pl.BlockSpec((2, PEER_NKEYS, PEER_QDIM // 2), lambda i, h: (0, 0, 0))],
        out_specs=[out, out],
        out_shape=[jax.ShapeDtypeStruct((PEER_HEADS * PEER_TOPK, t), jnp.int32),
                   jax.ShapeDtypeStruct((PEER_HEADS * PEER_TOPK, t), _F32)],
        compiler_params=_params(("parallel", "arbitrary")),
        name="pkeys",
    )(x1_bf, wq_bf, sub_keys)


def _peer_kernel(idx_ref, g_ref, x_ref, uv_hbm, grp_ref, grpt_ref, diag_ref, lg_ref, lb_ref,
                 o_ref, buf, arow, sem):
    nexp = PEER_HEADS * PEER_TOPK
    sub = uv_hbm.shape[1] // 2

    def row_copy(t, e, prio):
        src = uv_hbm.at[idx_ref[t * nexp + e]]
        return pltpu.make_async_copy(src, buf.at[t * nexp + e], sem.at[t])

    def issue(t, _):
        for e in range(nexp):
            row_copy(t, e, e % 2).start(priority=e % 2)
        return 0

    lax.fori_loop(0, PEER_TOK, issue, 0)

    diag = diag_ref[...]
    for t in range(PEER_TOK):
        pltpu.make_async_copy(uv_hbm.at[pl.ds(0, nexp)], buf.at[pl.ds(t * nexp, nexp)], sem.at[t]).wait()
        u = buf[pl.ds(t * nexp, nexp), 0:sub, :].reshape(nexp * sub, LANES)
        part = _nt_dot(x_ref[t].astype(_BF16), u) * diag
        arow[pl.ds(t, 1), :] = jnp.sum(part, axis=0, keepdims=True)

    a = arow[...]
    a_hi = a.astype(_BF16)
    a_lo = (a - a_hi.astype(_F32)).astype(_BF16)
    act = _dot(a_hi, grp_ref[...]) + _dot(a_lo, grp_ref[...])
    gelu = 0.5 * act * (1.0 + lax.erf(act * (2.0 ** -0.5)))
    w = (g_ref[...] * gelu).astype(_BF16)
    w_rep = _dot(w, grpt_ref[...])

    for t in range(PEER_TOK):
        v = buf[pl.ds(t * nexp, nexp), sub:2 * sub, :].reshape(nexp * sub, LANES)
        lhs = (w_rep[t:t + 1, :] * diag).astype(_BF16)
        h = DN_ALPHA * x_ref[t] + _dot(lhs, v)
        mu = jnp.mean(h)
        xc = h - mu
        var = jnp.mean(xc * xc)
        o_ref[t] = xc * lax.rsqrt(var + LN_EPS) * lg_ref[...] + lb_ref[...]


def _peer(idx_flat, g, x1r, uv, ln_g, ln_b):
    t, sub, _ = x1r.shape
    nexp = PEER_HEADS * PEER_TOPK
    d = sub * LANES
    pos = np.arange(nexp * sub)
    grp = (pos[:, None] // sub == np.arange(nexp)[None, :])
    diag = (np.arange(sub)[:, None] == pos[None, :] % sub)
    grp_bf = jnp.asarray(grp, _BF16)
    const2 = lambda shape: pl.BlockSpec(shape, lambda i: (0, 0))
    tok3 = pl.BlockSpec((PEER_TOK, sub, LANES), lambda i: (i, 0, 0))
    return pl.pallas_call(
        _peer_kernel,
        grid=(t // PEER_TOK,),
        in_specs=[pl.BlockSpec((PEER_TOK * nexp,), lambda i: (i,), memory_space=pltpu.SMEM),
                  pl.BlockSpec((PEER_TOK, nexp), lambda i: (i, 0)),
                  tok3,
                  pl.BlockSpec(memory_space=pl.ANY),
                  const2((nexp * sub, nexp)),
                  const2((nexp, nexp * sub)),
                  const2((sub, nexp * sub)),
                  const2((sub, LANES)),
                  const2((sub, LANES))],
        out_specs=tok3,
        out_shape=jax.ShapeDtypeStruct((t, sub, LANES), _F32),
        scratch_shapes=[pltpu.VMEM((PEER_TOK * nexp, 2 * sub, LANES), _BF16),
                        pltpu.VMEM((PEER_TOK, nexp * sub), _F32),
                        pltpu.SemaphoreType.DMA((PEER_TOK,))],
        compiler_params=_params(("arbitrary",)),
        name="peer",
    )(idx_flat, g, x1r, uv, grp_bf, grp_bf.T, jnp.asarray(diag, _F32),
      ln_g.reshape(sub, LANES), ln_b.reshape(sub, LANES))


def kernel(x, w_in, w_gate, b_gate, w_branch_sb, w_branch_moba, w_out, rel_bias, ln1_g, ln1_b,
           w_peer_query, peer_sub_keys, peer_u, peer_v, ln2_g, ln2_b):
    b, s, d = x.shape
    t = b * s
    x2 = x.reshape(t, d)
    x_bf = x2.astype(_BF16)
    sb_w = SB_HEADS * HEAD_DIM
    mb_w = MOBA_HEADS * HEAD_DIM

    proj, colmean = _proj(x_bf, w_in.astype(_BF16))
    proj3 = proj.reshape(b, s, proj.shape[1])
    nblk = s // MOBA_BLOCK
    kmean = colmean.reshape(b, nblk, -1)[:, :, 3 * sb_w + mb_w:3 * sb_w + 2 * mb_w]
    kmean_pad = jnp.pad(kmean, ((0, 0), (0, LANES - nblk), (0, 0)))

    row = np.arange(SB_BLOCK)[:, None]
    col = np.arange(2 * SB_BLOCK)[None, :]
    tri = jnp.asarray((row > col) | (col >= SB_BLOCK), _BF16)

    o_sb = _sb_attention(proj3, tri, 0)
    o_mb = _moba_attention(proj3, kmean_pad, rel_bias, 3 * sb_w // HEAD_DIM)
    mix = _mix(x_bf, o_sb.reshape(t, sb_w), o_mb.reshape(t, mb_w), w_gate.astype(_BF16), b_gate,
               w_branch_sb.astype(_BF16), w_branch_moba.astype(_BF16))
    x1, x1_bf = _outln(mix, w_out.astype(_BF16), x2, ln1_g, ln1_b)

    idx_t, g_t = _pkeys(x1_bf, w_peer_query.astype(_BF16), peer_sub_keys)
    sub = d // LANES
    n_exp = peer_u.shape[0]
    uv = jnp.concatenate([peer_u.astype(_BF16).reshape(n_exp, sub, LANES),
                          peer_v.astype(_BF16).reshape(n_exp, sub, LANES)], axis=1)
    out = _peer(idx_t.T.reshape(-1), g_t.T, x1.reshape(t, sub, LANES), uv, ln2_g, ln2_b)
    return out.reshape(b, s, d)
```

```python
import functools
import math

import numpy as np
import jax
import jax.numpy as jnp
from jax import lax
from jax.experimental import pallas as pl
from jax.experimental.pallas import tpu as pltpu

HEAD_DIM = 128
SB_HEADS = 8
MOBA_HEADS = 8
SB_BLOCK = 128
SB_TQ = 512
SB_GROUP = 1
SB_DEAD_LOG = -110.0
MOBA_BLOCK = 256
MOBA_TOPK = 3
MOBA_GROUP = 4
REL_BUCKETS = 32
REL_MAX_DIST = 1024
BIAS_TILES = 5
PEER_HEADS = 8
PEER_NKEYS = 128
PEER_QDIM = 256
PEER_TOPK = 16
LN_EPS = 1e-5
DN_ALPHA = 2.0 ** 0.25
NEG = -1e30
LANES = 128
PEER_TOK = 8
PEER_CHUNK = 16
VMEM_LIMIT = 56 * 1024 * 1024

_F32 = jnp.float32
_BF16 = jnp.bfloat16


def _nt_dot(a, b):
    return lax.dot_general(a, b, (((1,), (1,)), ((), ())), preferred_element_type=_F32)


def _dot(a, b):
    return jnp.dot(a, b, preferred_element_type=_F32)


def _params(sem):
    return pltpu.CompilerParams(dimension_semantics=sem, vmem_limit_bytes=VMEM_LIMIT)


def _proj_kernel(x_ref, w_ref, o_ref, m_ref):
    acc = _dot(x_ref[...], w_ref[...])
    o_ref[...] = acc.astype(o_ref.dtype)
    tm, tn = acc.shape
    nb = tm // MOBA_BLOCK
    m_ref[...] = jnp.mean(acc.reshape(nb, MOBA_BLOCK, tn), axis=1)[:, None, :]


def _proj(x_bf, w_bf, tm=1024, tn=512):
    t, d = x_bf.shape
    n = w_bf.shape[1]
    nb = tm // MOBA_BLOCK
    return pl.pallas_call(
        _proj_kernel,
        grid=(t // tm, n // tn),
        in_specs=[pl.BlockSpec((tm, d), lambda i, j: (i, 0)),
                  pl.BlockSpec((d, tn), lambda i, j: (0, j))],
        out_specs=[pl.BlockSpec((tm, tn), lambda i, j: (i, j)),
                   pl.BlockSpec((nb, 1, tn), lambda i, j: (i, 0, j))],
        out_shape=[jax.ShapeDtypeStruct((t, n), _BF16),
                   jax.ShapeDtypeStruct((t // MOBA_BLOCK, 1, n), _F32)],
        compiler_params=_params(("parallel", "arbitrary")),
        name="proj",
    )(x_bf, w_bf)


def _sb_kernel(q_ref, k_ref, v_ref, tri_ref, o_ref, c_ref, acc_ref, *, scale):
    i = pl.program_id(2)
    nsub = SB_TQ // SB_BLOCK
    tri = tri_ref[...]
    row = lax.broadcasted_iota(jnp.int32, (SB_TQ, SB_BLOCK), 0)
    col = lax.broadcasted_iota(jnp.int32, (SB_TQ, SB_BLOCK), 1)

    def tile(g, j, causal):
        lanes = slice(g * HEAD_DIM, (g + 1) * HEAD_DIM)
        start = pl.multiple_of(j * SB_BLOCK, SB_BLOCK)
        k = k_ref[0, pl.ds(start, SB_BLOCK), lanes]
        v = v_ref[0, pl.ds(start, SB_BLOCK), lanes]
        z = _nt_dot(q_ref[0, :, lanes], k) * scale
        lg = -(jnp.maximum(z, 0.0) + jnp.log1p(jnp.exp(-jnp.abs(z))))
        if causal is not None:
            lg = jnp.where(causal, lg, 0.0)
        lg_hi = lg.astype(_BF16)
        lg_lo = (lg - lg_hi.astype(_F32)).astype(_BF16)
        tr = _dot(lg_hi, tri) + _dot(lg_lo, tri)
        w = jnp.exp(z + lg + tr[:, :SB_BLOCK] + c_ref[g])
        if causal is not None:
            w = jnp.where(causal, w, 0.0)
        c_ref[g] += tr[:, SB_BLOCK:]
        acc_ref[g] += _dot(w.astype(_BF16), v)

    c_ref[...] = jnp.zeros_like(c_ref)
    acc_ref[...] = jnp.zeros_like(acc_ref)

    def diag_body(jj, carry):
        dj = nsub - 1 - jj
        causal = col + dj * SB_BLOCK < row
        for g in range(SB_GROUP):
            tile(g, nsub * i + dj, causal)
        return carry

    def live(carry):
        jj, log_rest = carry
        return jnp.logical_and(jj < nsub * i, log_rest > SB_DEAD_LOG)

    def body(carry):
        jj, _ = carry
        for g in range(SB_GROUP):
            tile(g, nsub * i - 1 - jj, None)
        return jj + 1, jnp.max(c_ref[...])

    lax.fori_loop(0, nsub, diag_body, 0)
    lax.while_loop(live, body, (jnp.int32(0), jnp.max(c_ref[...])))
    for g in range(SB_GROUP):
        o_ref[0, :, g * HEAD_DIM:(g + 1) * HEAD_DIM] = acc_ref[g].astype(o_ref.dtype)


def _sb_attention(proj3, tri, col0):
    b, s, _ = proj3.shape
    nq = s // SB_TQ
    gw = SB_GROUP * HEAD_DIM
    ng = SB_HEADS // SB_GROUP
    c0 = col0 // SB_GROUP
    return pl.pallas_call(
        functools.partial(_sb_kernel, scale=HEAD_DIM ** -0.5),
        grid=(b, ng, nq),
        in_specs=[pl.BlockSpec((1, SB_TQ, gw), lambda bi, h, i: (bi, i, c0 + h)),
                  pl.BlockSpec((1, s, gw), lambda bi, h, i: (bi, 0, c0 + ng + h)),
                  pl.BlockSpec((1, s, gw), lambda bi, h, i: (bi, 0, c0 + 2 * ng + h)),
                  pl.BlockSpec((SB_BLOCK, 2 * SB_BLOCK), lambda bi, h, i: (0, 0))],
        out_specs=pl.BlockSpec((1, SB_TQ, gw), lambda bi, h, i: (bi, i, h)),
        out_shape=jax.ShapeDtypeStruct((b, s, SB_HEADS * HEAD_DIM), _BF16),
        scratch_shapes=[pltpu.VMEM((SB_GROUP, SB_TQ, HEAD_DIM), _F32),
                        pltpu.VMEM((SB_GROUP, SB_TQ, HEAD_DIM), _F32)],
        compiler_params=_params(("parallel", "parallel", "arbitrary")),
        name="sb_attn",
    )(proj3, proj3, proj3, tri)


def _rel_bucket_table():
    n_exact = REL_BUCKETS // 2
    d = np.arange(BIAS_TILES)[:, None, None]
    q = np.arange(MOBA_BLOCK)[None, :, None]
    k = np.arange(MOBA_BLOCK)[None, None, :]
    rel = np.maximum(d * MOBA_BLOCK + q - k, 0)
    logd = (np.log(np.maximum(rel, 1).astype(np.float32) / np.float32(n_exact))
            / np.float32(math.log(REL_MAX_DIST / n_exact))).astype(np.float32)
    large = n_exact + (logd * np.float32(REL_BUCKETS - n_exact)).astype(np.int32)
    large = np.minimum(large, REL_BUCKETS - 1)
    return np.where(rel < n_exact, rel, large).astype(np.int32)


def _moba_kernel(rb_ref, q_ref, k_ref, v_ref, km_ref, bkt_ref, o_ref, bias_ref, *, scale):
    h = pl.program_id(0)
    first = jnp.logical_and(pl.program_id(1) == 0, pl.program_id(2) == 0)
    i = pl.program_id(2)

    @pl.when(first)
    def _():
        for d in range(BIAS_TILES):
            bkt = bkt_ref[d]
            tile = jnp.zeros((MOBA_BLOCK, MOBA_BLOCK), _F32)
            for bi in range(REL_BUCKETS):
                tile = jnp.where(bkt == bi, rb_ref[h, bi], tile)
            bias_ref[d] = tile
        bias_ref[BIAS_TILES] = jnp.full((MOBA_BLOCK, MOBA_BLOCK), rb_ref[h, REL_BUCKETS - 1], _F32)

    q = q_ref[0]
    lane = lax.broadcasted_iota(jnp.int32, (MOBA_BLOCK, LANES), 1)
    valid = lane < i
    gate = _nt_dot(q, km_ref[0].astype(_BF16))
    g = jnp.where(valid, gate, -jnp.inf)
    sel = jnp.zeros((MOBA_BLOCK, LANES), jnp.bool_)
    for _ in range(MOBA_TOPK):
        m = jnp.max(g, axis=-1, keepdims=True)
        first_idx = jnp.min(jnp.where(g == m, lane, LANES), axis=-1, keepdims=True)
        pick = lane == first_idx
        sel = jnp.logical_or(sel, pick)
        g = jnp.where(pick, -jnp.inf, g)
    sel_f = jnp.where(jnp.logical_and(sel, valid), 1.0, 0.0)

    row = lax.broadcasted_iota(jnp.int32, (MOBA_BLOCK, MOBA_BLOCK), 0)
    col = lax.broadcasted_iota(jnp.int32, (MOBA_BLOCK, MOBA_BLOCK), 1)

    def block(j):
        start = pl.multiple_of(j * MOBA_BLOCK, MOBA_BLOCK)
        return k_ref[0, pl.ds(start, MOBA_BLOCK), :], v_ref[0, pl.ds(start, MOBA_BLOCK), :]

    kb, vb = block(i)
    s = _nt_dot(q, kb) * scale + bias_ref[0]
    s = jnp.where(col <= row, s, NEG)
    m0 = jnp.max(s, axis=-1, keepdims=True)
    p = jnp.exp(s - m0)
    l0 = jnp.sum(p, axis=-1, keepdims=True)
    acc0 = _dot(p.astype(_BF16), vb)

    def body(jg, carry):
        m, l, acc = carry
        scores, values = [], []
        for u in range(MOBA_GROUP):
            j = jg * MOBA_GROUP + u
            kb, vb = block(j)
            d = jnp.clip(i - j, 0, BIAS_TILES)
            s = _nt_dot(q, kb) * scale + bias_ref[d]
            chosen = jnp.sum(jnp.where(lane == j, sel_f, 0.0), axis=-1, keepdims=True)
            scores.append(jnp.where(chosen > 0.5, s, NEG))
            values.append(vb)
        top = functools.reduce(jnp.maximum, scores)
        m_new = jnp.maximum(m, jnp.max(top, axis=-1, keepdims=True))
        a = jnp.exp(m - m_new)
        probs = [jnp.exp(s - m_new) for s in scores]
        l = a * l + jnp.sum(functools.reduce(jnp.add, probs), axis=-1, keepdims=True)
        acc = a * acc + functools.reduce(jnp.add, [_dot(p.astype(_BF16), vb) for p, vb in zip(probs, values)])
        return m_new, l, acc

    ngroups = (i + MOBA_GROUP - 1) // MOBA_GROUP
    m, l, acc = lax.fori_loop(0, ngroups, body, (m0, l0, acc0))
    o_ref[0] = (acc / l).astype(o_ref.dtype)


def _moba_attention(proj3, kmean_pad, rel_bias, col0):
    b, s, _ = proj3.shape
    nq = s // MOBA_BLOCK
    assert nq % MOBA_GROUP == 0 and nq <= LANES
    bkt = jnp.asarray(_rel_bucket_table())
    return pl.pallas_call(
        functools.partial(_moba_kernel, scale=HEAD_DIM ** -0.5),
        grid=(MOBA_HEADS, b, nq),
        in_specs=[pl.BlockSpec(memory_space=pltpu.SMEM),
                  pl.BlockSpec((1, MOBA_BLOCK, HEAD_DIM), lambda h, bi, i: (bi, i, col0 + h)),
                  pl.BlockSpec((1, s, HEAD_DIM), lambda h, bi, i: (bi, 0, col0 + MOBA_HEADS + h)),
                  pl.BlockSpec((1, s, HEAD_DIM), lambda h, bi, i: (bi, 0, col0 + 2 * MOBA_HEADS + h)),
                  pl.BlockSpec((1, LANES, HEAD_DIM), lambda h, bi, i: (bi, 0, h)),
                  pl.BlockSpec((BIAS_TILES, MOBA_BLOCK, MOBA_BLOCK), lambda h, bi, i: (0, 0, 0))],
        out_specs=pl.BlockSpec((1, MOBA_BLOCK, HEAD_DIM), lambda h, bi, i: (bi, i, h)),
        out_shape=jax.ShapeDtypeStruct((b, s, MOBA_HEADS * HEAD_DIM), _BF16),
        scratch_shapes=[pltpu.VMEM((BIAS_TILES + 1, MOBA_BLOCK, MOBA_BLOCK), _F32)],
        compiler_params=_params(("arbitrary", "arbitrary", "arbitrary")),
        name="moba_attn",
    )(rel_bias, proj3, proj3, proj3, kmean_pad, bkt)


def _mix_kernel(x_ref, osb_ref, omb_ref, wg1_ref, wg2_ref, b1_ref, b2_ref, wsb_ref, wmb_ref, o_ref):
    x = x_ref[...]
    g_sb = jax.nn.sigmoid(_dot(x, wg1_ref[...]) + b1_ref[...])
    g_mb = jax.nn.sigmoid(_dot(x, wg2_ref[...]) + b2_ref[...])
    y_sb = _dot(osb_ref[...], wsb_ref[...])
    y_mb = _dot(omb_ref[...], wmb_ref[...])
    o_ref[...] = (g_sb * y_sb + g_mb * y_mb).astype(o_ref.dtype)


def _mix(x_bf, o_sb, o_mb, wg_bf, b_gate, wsb_bf, wmb_bf, tm=1024, tn=512):
    t, d = x_bf.shape
    w = o_sb.shape[1]
    nj = d // tn
    b2d = b_gate.reshape(1, 2 * d)
    return pl.pallas_call(
        _mix_kernel,
        grid=(t // tm, nj),
        in_specs=[pl.BlockSpec((tm, d), lambda i, j: (i, 0)),
                  pl.BlockSpec((tm, w), lambda i, j: (i, 0)),
                  pl.BlockSpec((tm, w), lambda i, j: (i, 0)),
                  pl.BlockSpec((d, tn), lambda i, j: (0, j)),
                  pl.BlockSpec((d, tn), lambda i, j: (0, nj + j)),
                  pl.BlockSpec((1, tn), lambda i, j: (0, j)),
                  pl.BlockSpec((1, tn), lambda i, j: (0, nj + j)),
                  pl.BlockSpec((w, tn), lambda i, j: (0, j)),
                  pl.BlockSpec((w, tn), lambda i, j: (0, j))],
        out_specs=pl.BlockSpec((tm, tn), lambda i, j: (i, j)),
        out_shape=jax.ShapeDtypeStruct((t, d), _BF16),
        compiler_params=_params(("parallel", "arbitrary")),
        name="mix",
    )(x_bf, o_sb, o_mb, wg_bf, wg_bf, b2d, b2d, wsb_bf, wmb_bf)


def _layer_norm(h, g, b):
    mu = jnp.mean(h, axis=-1, keepdims=True)
    xc = h - mu
    var = jnp.mean(xc * xc, axis=-1, keepdims=True)
    return xc * lax.rsqrt(var + LN_EPS) * g + b


def _outln_kernel(m_ref, w_ref, x_ref, g_ref, b_ref, o_ref, obf_ref):
    h = DN_ALPHA * x_ref[...] + _dot(m_ref[...], w_ref[...])
    y = _layer_norm(h, g_ref[...], b_ref[...])
    o_ref[...] = y
    obf_ref[...] = y.astype(_BF16)


def _outln(mix_bf, w_bf, x2, g, b, tm=256):
    t, d = x2.shape
    row = pl.BlockSpec((tm, d), lambda i: (i, 0))
    vec = pl.BlockSpec((1, d), lambda i: (0, 0))
    return pl.pallas_call(
        _outln_kernel,
        grid=(t // tm,),
        in_specs=[row, pl.BlockSpec((d, d), lambda i: (0, 0)), row, vec, vec],
        out_specs=[row, row],
        out_shape=[jax.ShapeDtypeStruct((t, d), _F32), jax.ShapeDtypeStruct((t, d), _BF16)],
        compiler_params=_params(("parallel",)),
        name="outln",
    )(mix_bf, w_bf, x2, g.reshape(1, d), b.reshape(1, d))


def _pkeys_kernel(x_ref, w_ref, sk_ref, idx_ref, g_ref):
    tm = x_ref.shape[0]
    k = PEER_TOPK
    nk = PEER_NKEYS
    half = PEER_QDIM // 2
    inf = jnp.inf
    pq = _dot(x_ref[...], w_ref[...]).astype(_BF16)
    key_id = lax.broadcasted_iota(jnp.int32, (nk, tm), 0).astype(_F32)
    rank = lax.broadcasted_iota(jnp.int32, (k, tm), 0)

    def first_max(pieces, ids):
        m = pieces[0]
        for piece in pieces[1:]:
            m = jnp.maximum(m, piece)
        m = jnp.max(m, axis=0, keepdims=True)
        f = None
        for piece, pid in zip(pieces, ids):
            c = jnp.where(piece == m, pid, 1e9)
            f = c if f is None else jnp.minimum(f, c)
        return m, jnp.min(f, axis=0, keepdims=True)

    top_val, top_key = [], []
    for p in range(2):
        s = _nt_dot(sk_ref[p].astype(_BF16), pq[:, p * half:(p + 1) * half])
        val = jnp.zeros((k, tm), _F32)
        key = jnp.zeros((k, tm), _F32)
        for r in range(k):
            m, f = first_max([s], [key_id])
            val = jnp.where(rank == r, m, val)
            key = jnp.where(rank == r, f, key)
            s = jnp.where(key_id == f, -inf, s)
        top_val.append(val)
        top_key.append(key)

    rank_f = rank.astype(_F32)
    cand = [top_val[0][a:a + 1] + top_val[1] for a in range(k)]
    expert = [top_key[0][a:a + 1] * float(nk) + top_key[1] for a in range(k)]
    flat = [rank_f + float(a * k) for a in range(k)]
    best = jnp.zeros((k, tm), _F32)
    ids = jnp.zeros((k, tm), _F32)
    for r in range(k):
        m, f = first_max(cand, flat)
        e = None
        for a in range(k):
            hit = flat[a] == f
            ea = jnp.where(hit, expert[a], -1.0)
            e = ea if e is None else jnp.maximum(e, ea)
            cand[a] = jnp.where(hit, -inf, cand[a])
        best = jnp.where(rank == r, m, best)
        ids = jnp.where(rank == r, jnp.max(e, axis=0, keepdims=True), ids)
    w = jnp.exp(best - best[0:1])
    idx_ref[...] = ids.astype(jnp.int32)
    g_ref[...] = w / jnp.sum(w, axis=0, keepdims=True)


def _pkeys(x1_bf, wq_bf, sub_keys, tm=256):
    t, d = x1_bf.shape
    out = pl.BlockSpec((PEER_TOPK, tm), lambda i, h: (h, i))
    return pl.pallas_call(
        _pkeys_kernel,
        grid=(t // tm, PEER_HEADS),
        in_specs=[pl.BlockSpec((tm, d), lambda i, h: (i, 0)),
                  pl.BlockSpec((d, PEER_QDIM), lambda i, h: (0, h)),
                  pl.BlockSpec((2, PEER_NKEYS, PEER_QDIM // 2), lambda i, h: (0, 0, 0))],
        out_specs=[out, out],
        out_shape=[jax.ShapeDtypeStruct((PEER_HEADS * PEER_TOPK, t), jnp.int32),
                   jax.ShapeDtypeStruct((PEER_HEADS * PEER_TOPK, t), _F32)],
        compiler_params=_params(("parallel", "arbitrary")),
        name="pkeys",
    )(x1_bf, wq_bf, sub_keys)


def _peer_kernel(idx_cur, idx_nxt, g_ref, x_ref, xp_ref, uv_hbm, grp_ref, diag_ref, lg_ref, lb_ref,
                 o_ref, buf0, buf1, buf2, wl0, wl1, wl2, arow, sem):
    nexp = PEER_HEADS * PEER_TOPK
    sub = uv_hbm.shape[1] // 2
    nchunk = nexp // PEER_CHUNK
    s = pl.program_id(0)
    last = pl.num_programs(0) - 1
    bufs = (buf0, buf1, buf2)
    wls = (wl0, wl1, wl2)

    def start(slot, idx_ref, t, e):
        src = uv_hbm.at[idx_ref[t * nexp + e]]
        pltpu.make_async_copy(src, bufs[slot].at[t * nexp + e], sem.at[slot, t]).start(priority=e % 2)

    def wait(slot, t):
        pltpu.make_async_copy(uv_hbm.at[pl.ds(0, nexp)], bufs[slot].at[pl.ds(t * nexp, nexp)],
                              sem.at[slot, t]).wait()

    @pl.when(s == 0)
    def _():
        wl2[...] = jnp.zeros_like(wl2)

        def fill(t, carry):
            buf2[pl.ds(t * nexp, nexp)] = jnp.zeros((nexp,) + buf2.shape[1:], buf2.dtype)
            for e in range(nexp):
                start(0, idx_cur, t, e)
            return carry

        lax.fori_loop(0, PEER_TOK, fill, 0)

    def step(cur):
        nxt = (cur + 1) % 3
        prv = (cur + 2) % 3
        diag = diag_ref[...]
        rows = []
        for t in range(PEER_TOK):
            wait(cur, t)
            xb = x_ref[t].astype(_BF16)
            acc = [jnp.zeros((sub, LANES), _F32) for _ in range(4)]
            for c in range(nchunk):
                base = t * nexp + c * PEER_CHUNK
                for e in range(PEER_CHUNK):
                    start(nxt, idx_nxt, t, c * PEER_CHUNK + e)
                u = bufs[cur][base:base + PEER_CHUNK, 0:sub, :].reshape(PEER_CHUNK * sub, LANES)
                part = _nt_dot(xb, u) * diag
                arow[t:t + 1, c * PEER_CHUNK * sub:(c + 1) * PEER_CHUNK * sub] = jnp.sum(part, axis=0,
                                                                                     keepdims=True)
                v = bufs[prv][base:base + PEER_CHUNK, sub:2 * sub, :]
                for e in range(PEER_CHUNK):
                    acc[e % 4] = acc[e % 4] + v[e].astype(_F32) * wls[prv][base + e:base + e + 1, :]
            rows.append(DN_ALPHA * xp_ref[t] + ((acc[0] + acc[1]) + (acc[2] + acc[3])))

        h = jnp.stack(rows)
        inv_d = 1.0 / (sub * LANES)
        mu = jnp.sum(jnp.sum(h, axis=2, keepdims=True), axis=1, keepdims=True) * inv_d
        xc = h - mu
        var = jnp.sum(jnp.sum(xc * xc, axis=2, keepdims=True), axis=1, keepdims=True) * inv_d
        o_ref[...] = xc * lax.rsqrt(var + LN_EPS) * lg_ref[...] + lb_ref[...]

        a = arow[...]
        a_hi = a.astype(_BF16)
        a_lo = (a - a_hi.astype(_F32)).astype(_BF16)
        act = _dot(a_hi, grp_ref[...]) + _dot(a_lo, grp_ref[...])
        gelu = 0.5 * act * (1.0 + lax.erf(act * (2.0 ** -0.5)))
        w = g_ref[...] * gelu
        eye = (lax.broadcasted_iota(jnp.int32, (nexp, nexp), 0)
               == lax.broadcasted_iota(jnp.int32, (nexp, nexp), 1))
        ones = jnp.ones((nexp, LANES), _BF16)
        for t in range(PEER_TOK):
            w_diag = jnp.where(eye, w[t:t + 1, :], 0.0).astype(_BF16)
            wls[cur][t * nexp:(t + 1) * nexp, :] = _dot(w_diag, ones)

        @pl.when(s == last)
        def _():
            for t in range(PEER_TOK):
                wait(nxt, t)

    for cur in range(3):
        pl.when(lax.rem(s, 3) == cur)(functools.partial(step, cur))


def _peer(idx_flat, g, x1r, uv, ln_g, ln_b):
    t, sub, _ = x1r.shape
    nexp = PEER_HEADS * PEER_TOPK
    n = t // PEER_TOK
    pos = np.arange(nexp * sub)
    grp = jnp.asarray(pos[:, None] // sub == np.arange(nexp)[None, :], _BF16)
    diag = jnp.asarray(np.arange(sub)[:, None] == np.arange(PEER_CHUNK * sub)[None, :] % sub, _F32)
    const2 = lambda shape: pl.BlockSpec(shape, lambda i: (0, 0))
    this = lambda i: jnp.minimum(i, n - 1)
    prev = lambda i: jnp.maximum(i - 1, 0)
    following = lambda i: jnp.minimum(i + 1, n - 1)
    tok3 = lambda at: pl.BlockSpec((PEER_TOK, sub, LANES), lambda i: (at(i), 0, 0))
    ids = lambda at: pl.BlockSpec((PEER_TOK * nexp,), lambda i: (at(i),), memory_space=pltpu.SMEM)
    return pl.pallas_call(
        _peer_kernel,
        grid=(n + 1,),
        in_specs=[ids(this), ids(following),
                  pl.BlockSpec((PEER_TOK, nexp), lambda i: (this(i), 0)),
                  tok3(this), tok3(prev),
                  pl.BlockSpec(memory_space=pl.ANY),
                  const2((nexp * sub, nexp)),
                  const2((sub, PEER_CHUNK * sub)),
                  const2((sub, LANES)),
                  const2((sub, LANES))],
        out_specs=tok3(prev),
        out_shape=jax.ShapeDtypeStruct((t, sub, LANES), _F32),
        scratch_shapes=[pltpu.VMEM((PEER_TOK * nexp, 2 * sub, LANES), _BF16)] * 3
        + [pltpu.VMEM((PEER_TOK * nexp, LANES), _F32)] * 3
        + [pltpu.VMEM((PEER_TOK, nexp * sub), _F32), pltpu.SemaphoreType.DMA((3, PEER_TOK))],
        compiler_params=_params(("arbitrary",)),
        name="peer",
    )(idx_flat, idx_flat, g, x1r, x1r, uv, grp, diag, ln_g.reshape(sub, LANES), ln_b.reshape(sub, LANES))


def kernel(x, w_in, w_gate, b_gate, w_branch_sb, w_branch_moba, w_out, rel_bias, ln1_g, ln1_b,
           w_peer_query, peer_sub_keys, peer_u, peer_v, ln2_g, ln2_b):
    b, s, d = x.shape
    t = b * s
    x2 = x.reshape(t, d)
    x_bf = x2.astype(_BF16)
    sb_w = SB_HEADS * HEAD_DIM
    mb_w = MOBA_HEADS * HEAD_DIM

    proj, colmean = _proj(x_bf, w_in.astype(_BF16))
    proj3 = proj.reshape(b, s, proj.shape[1])
    nblk = s // MOBA_BLOCK
    kmean = colmean.reshape(b, nblk, -1)[:, :, 3 * sb_w + mb_w:3 * sb_w + 2 * mb_w]
    kmean_pad = jnp.pad(kmean, ((0, 0), (0, LANES - nblk), (0, 0)))

    row = np.arange(SB_BLOCK)[:, None]
    col = np.arange(2 * SB_BLOCK)[None, :]
    tri = jnp.asarray((row > col) | (col >= SB_BLOCK), _BF16)

    o_sb = _sb_attention(proj3, tri, 0)
    o_mb = _moba_attention(proj3, kmean_pad, rel_bias, 3 * sb_w // HEAD_DIM)
    mix = _mix(x_bf, o_sb.reshape(t, sb_w), o_mb.reshape(t, mb_w), w_gate.astype(_BF16), b_gate,
               w_branch_sb.astype(_BF16), w_branch_moba.astype(_BF16))
    x1, x1_bf = _outln(mix, w_out.astype(_BF16), x2, ln1_g, ln1_b)

    idx_t, g_t = _pkeys(x1_bf, w_peer_query.astype(_BF16), peer_sub_keys)
    sub = d // LANES
    n_exp = peer_u.shape[0]
    uv = jnp.concatenate([peer_u.astype(_BF16).reshape(n_exp, sub, LANES),
                          peer_v.astype(_BF16).reshape(n_exp, sub, LANES)], axis=1)
    out = _peer(idx_t.T.reshape(-1), g_t.T, x1.reshape(t, sub, LANES), uv, ln2_g, ln2_b)
    return out.reshape(b, s, d)
```

```python
import functools
import math

import numpy as np
import jax
import jax.numpy as jnp
from jax import lax
from jax.experimental import pallas as pl
from jax.experimental.pallas import tpu as pltpu

HEAD_DIM = 128
SB_HEADS = 8
MOBA_HEADS = 8
SB_BLOCK = 128
SB_TQ = 512
SB_GROUP = 1
SB_DEAD_LOG = -110.0
MOBA_BLOCK = 256
MOBA_TOPK = 3
MOBA_GROUP = 4
REL_BUCKETS = 32
REL_MAX_DIST = 1024
BIAS_TILES = 5
PEER_HEADS = 8
PEER_NKEYS = 128
PEER_QDIM = 256
PEER_TOPK = 16
LN_EPS = 1e-5
DN_ALPHA = 2.0 ** 0.25
NEG = -1e30
LANES = 128
PEER_TOK = 8
PEER_CHUNK = 16
PEER_STORE_LAG = 3
PEER_ISSUE = (8, 64)
VMEM_LIMIT = 56 * 1024 * 1024

_F32 = jnp.float32
_BF16 = jnp.bfloat16


def _nt_dot(a, b):
    return lax.dot_general(a, b, (((1,), (1,)), ((), ())), preferred_element_type=_F32)


def _dot(a, b):
    return jnp.dot(a, b, preferred_element_type=_F32)


def _params(sem):
    return pltpu.CompilerParams(dimension_semantics=sem, vmem_limit_bytes=VMEM_LIMIT)


def _proj_kernel(x_ref, w_ref, o_ref, m_ref):
    acc = _dot(x_ref[...], w_ref[...])
    o_ref[...] = acc.astype(o_ref.dtype)
    tm, tn = acc.shape
    nb = tm // MOBA_BLOCK
    m_ref[...] = jnp.mean(acc.reshape(nb, MOBA_BLOCK, tn), axis=1)[:, None, :]


def _proj(x_bf, w_bf, tm=1024, tn=512):
    t, d = x_bf.shape
    n = w_bf.shape[1]
    nb = tm // MOBA_BLOCK
    return pl.pallas_call(
        _proj_kernel,
        grid=(t // tm, n // tn),
        in_specs=[pl.BlockSpec((tm, d), lambda i, j: (i, 0)),
                  pl.BlockSpec((d, tn), lambda i, j: (0, j))],
        out_specs=[pl.BlockSpec((tm, tn), lambda i, j: (i, j)),
                   pl.BlockSpec((nb, 1, tn), lambda i, j: (i, 0, j))],
        out_shape=[jax.ShapeDtypeStruct((t, n), _BF16),
                   jax.ShapeDtypeStruct((t // MOBA_BLOCK, 1, n), _F32)],
        compiler_params=_params(("parallel", "arbitrary")),
        name="proj",
    )(x_bf, w_bf)


def _sb_kernel(q_ref, k_ref, v_ref, tri_ref, o_ref, c_ref, acc_ref, *, scale):
    i = pl.program_id(2)
    nsub = SB_TQ // SB_BLOCK
    tri = tri_ref[...]
    row = lax.broadcasted_iota(jnp.int32, (SB_TQ, SB_BLOCK), 0)
    col = lax.broadcasted_iota(jnp.int32, (SB_TQ, SB_BLOCK), 1)

    def blocks(g, j_top, causals):
        lanes = slice(g * HEAD_DIM, (g + 1) * HEAD_DIM)
        q = q_ref[0, :, lanes]
        parts = []
        for u, causal in enumerate(causals):
            start = pl.multiple_of((j_top - u) * SB_BLOCK, SB_BLOCK)
            k = k_ref[0, pl.ds(start, SB_BLOCK), lanes]
            z = _nt_dot(q, k) * scale
            lg = -(jnp.maximum(z, 0.0) + jnp.log1p(jnp.exp(-jnp.abs(z))))
            if causal is not None:
                lg = jnp.where(causal, lg, 0.0)
            lg_hi = lg.astype(_BF16)
            lg_lo = (lg - lg_hi.astype(_F32)).astype(_BF16)
            tr = _dot(lg_hi, tri) + _dot(lg_lo, tri)
            parts.append((start, z + lg, tr, causal))
        c = c_ref[g]
        pv = None
        for start, log_beta, tr, causal in parts:
            w = jnp.exp(log_beta + tr[:, :SB_BLOCK] + c)
            if causal is not None:
                w = jnp.where(causal, w, 0.0)
            d = _dot(w.astype(_BF16), v_ref[0, pl.ds(start, SB_BLOCK), lanes])
            pv = d if pv is None else pv + d
            c = c + tr[:, SB_BLOCK:]
        c_ref[g] = c
        acc_ref[g] += pv

    c_ref[...] = jnp.zeros_like(c_ref)
    acc_ref[...] = jnp.zeros_like(acc_ref)

    diag_masks = [col + dj * SB_BLOCK < row for dj in range(nsub - 1, -1, -1)]
    for g in range(SB_GROUP):
        blocks(g, nsub * i + nsub - 1, diag_masks)

    def live(carry):
        jj, log_rest = carry
        return jnp.logical_and(jj < i, log_rest > SB_DEAD_LOG)

    def body(carry):
        jj, _ = carry
        for g in range(SB_GROUP):
            blocks(g, nsub * (i - jj) - 1, [None] * nsub)
        return jj + 1, jnp.max(c_ref[...])

    lax.while_loop(live, body, (jnp.int32(0), jnp.max(c_ref[...])))
    for g in range(SB_GROUP):
        o_ref[0, :, g * HEAD_DIM:(g + 1) * HEAD_DIM] = acc_ref[g].astype(o_ref.dtype)


def _sb_attention(proj3, tri, col0):
    b, s, _ = proj3.shape
    nq = s // SB_TQ
    gw = SB_GROUP * HEAD_DIM
    ng = SB_HEADS // SB_GROUP
    c0 = col0 // SB_GROUP
    return pl.pallas_call(
        functools.partial(_sb_kernel, scale=HEAD_DIM ** -0.5),
        grid=(b, ng, nq),
        in_specs=[pl.BlockSpec((1, SB_TQ, gw), lambda bi, h, i: (bi, i, c0 + h)),
                  pl.BlockSpec((1, s, gw), lambda bi, h, i: (bi, 0, c0 + ng + h)),
                  pl.BlockSpec((1, s, gw), lambda bi, h, i: (bi, 0, c0 + 2 * ng + h)),
                  pl.BlockSpec((SB_BLOCK, 2 * SB_BLOCK), lambda bi, h, i: (0, 0))],
        out_specs=pl.BlockSpec((1, SB_TQ, gw), lambda bi, h, i: (bi, i, h)),
        out_shape=jax.ShapeDtypeStruct((b, s, SB_HEADS * HEAD_DIM), _BF16),
        scratch_shapes=[pltpu.VMEM((SB_GROUP, SB_TQ, HEAD_DIM), _F32),
                        pltpu.VMEM((SB_GROUP, SB_TQ, HEAD_DIM), _F32)],
        compiler_params=_params(("parallel", "parallel", "arbitrary")),
        name="sb_attn",
    )(proj3, proj3, proj3, tri)


def _rel_bucket_table():
    n_exact = REL_BUCKETS // 2
    d = np.arange(BIAS_TILES)[:, None, None]
    q = np.arange(MOBA_BLOCK)[None, :, None]
    k = np.arange(MOBA_BLOCK)[None, None, :]
    rel = np.maximum(d * MOBA_BLOCK + q - k, 0)
    logd = (np.log(np.maximum(rel, 1).astype(np.float32) / np.float32(n_exact))
            / np.float32(math.log(REL_MAX_DIST / n_exact))).astype(np.float32)
    large = n_exact + (logd * np.float32(REL_BUCKETS - n_exact)).astype(np.int32)
    large = np.minimum(large, REL_BUCKETS - 1)
    return np.where(rel < n_exact, rel, large).astype(np.int32)


def _moba_kernel(rb_ref, q_ref, k_ref, v_ref, km_ref, bkt_ref, o_ref, bias_ref, *, scale):
    h = pl.program_id(0)
    first = jnp.logical_and(pl.program_id(1) == 0, pl.program_id(2) == 0)
    i = pl.program_id(2)

    @pl.when(first)
    def _():
        for d in range(BIAS_TILES):
            bkt = bkt_ref[d]
            tile = jnp.zeros((MOBA_BLOCK, MOBA_BLOCK), _F32)
            for bi in range(REL_BUCKETS):
                tile = jnp.where(bkt == bi, rb_ref[h, bi], tile)
            bias_ref[d] = tile
        bias_ref[BIAS_TILES] = jnp.full((MOBA_BLOCK, MOBA_BLOCK), rb_ref[h, REL_BUCKETS - 1], _F32)

    q = q_ref[0]
    lane = lax.broadcasted_iota(jnp.int32, (MOBA_BLOCK, LANES), 1)
    valid = lane < i
    gate = _nt_dot(q, km_ref[0].astype(_BF16))
    g = jnp.where(valid, gate, -jnp.inf)
    sel = jnp.zeros((MOBA_BLOCK, LANES), jnp.bool_)
    for _ in range(MOBA_TOPK):
        m = jnp.max(g, axis=-1, keepdims=True)
        first_idx = jnp.min(jnp.where(g == m, lane, LANES), axis=-1, keepdims=True)
        pick = lane == first_idx
        sel = jnp.logical_or(sel, pick)
        g = jnp.where(pick, -jnp.inf, g)
    sel_f = jnp.where(jnp.logical_and(sel, valid), 1.0, 0.0)

    row = lax.broadcasted_iota(jnp.int32, (MOBA_BLOCK, MOBA_BLOCK), 0)
    col = lax.broadcasted_iota(jnp.int32, (MOBA_BLOCK, MOBA_BLOCK), 1)

    def block(j):
        start = pl.multiple_of(j * MOBA_BLOCK, MOBA_BLOCK)
        return k_ref[0, pl.ds(start, MOBA_BLOCK), :], v_ref[0, pl.ds(start, MOBA_BLOCK), :]

    kb, vb = block(i)
    s = _nt_dot(q, kb) * scale + bias_ref[0]
    s = jnp.where(col <= row, s, NEG)
    m0 = jnp.max(s, axis=-1, keepdims=True)
    p = jnp.exp(s - m0)
    l0 = jnp.sum(p, axis=-1, keepdims=True)
    acc0 = _dot(p.astype(_BF16), vb)

    def body(jg, carry):
        m, l, acc = carry
        scores, values = [], []
        for u in range(MOBA_GROUP):
            j = jg * MOBA_GROUP + u
            kb, vb = block(j)
            d = jnp.clip(i - j, 0, BIAS_TILES)
            s = _nt_dot(q, kb) * scale + bias_ref[d]
            chosen = jnp.sum(jnp.where(lane == j, sel_f, 0.0), axis=-1, keepdims=True)
            scores.append(jnp.where(chosen > 0.5, s, NEG))
            values.append(vb)
        top = functools.reduce(jnp.maximum, scores)
        m_new = jnp.maximum(m, jnp.max(top, axis=-1, keepdims=True))
        a = jnp.exp(m - m_new)
        probs = [jnp.exp(s - m_new) for s in scores]
        l = a * l + jnp.sum(functools.reduce(jnp.add, probs), axis=-1, keepdims=True)
        acc = a * acc + functools.reduce(jnp.add, [_dot(p.astype(_BF16), vb) for p, vb in zip(probs, values)])
        return m_new, l, acc

    ngroups = (i + MOBA_GROUP - 1) // MOBA_GROUP
    m, l, acc = lax.fori_loop(0, ngroups, body, (m0, l0, acc0))
    o_ref[0] = (acc / l).astype(o_ref.dtype)


def _moba_attention(proj3, kmean_pad, rel_bias, col0):
    b, s, _ = proj3.shape
    nq = s // MOBA_BLOCK
    assert nq % MOBA_GROUP == 0 and nq <= LANES
    bkt = jnp.asarray(_rel_bucket_table())
    return pl.pallas_call(
        functools.partial(_moba_kernel, scale=HEAD_DIM ** -0.5),
        grid=(MOBA_HEADS, b, nq),
        in_specs=[pl.BlockSpec(memory_space=pltpu.SMEM),
                  pl.BlockSpec((1, MOBA_BLOCK, HEAD_DIM), lambda h, bi, i: (bi, i, col0 + h)),
                  pl.BlockSpec((1, s, HEAD_DIM), lambda h, bi, i: (bi, 0, col0 + MOBA_HEADS + h)),
                  pl.BlockSpec((1, s, HEAD_DIM), lambda h, bi, i: (bi, 0, col0 + 2 * MOBA_HEADS + h)),
                  pl.BlockSpec((1, LANES, HEAD_DIM), lambda h, bi, i: (bi, 0, h)),
                  pl.BlockSpec((BIAS_TILES, MOBA_BLOCK, MOBA_BLOCK), lambda h, bi, i: (0, 0, 0))],
        out_specs=pl.BlockSpec((1, MOBA_BLOCK, HEAD_DIM), lambda h, bi, i: (bi, i, h)),
        out_shape=jax.ShapeDtypeStruct((b, s, MOBA_HEADS * HEAD_DIM), _BF16),
        scratch_shapes=[pltpu.VMEM((BIAS_TILES + 1, MOBA_BLOCK, MOBA_BLOCK), _F32)],
        compiler_params=_params(("arbitrary", "arbitrary", "arbitrary")),
        name="moba_attn",
    )(rel_bias, proj3, proj3, proj3, kmean_pad, bkt)


def _mix_kernel(x_ref, osb_ref, omb_ref, wg1_ref, wg2_ref, b1_ref, b2_ref, wsb_ref, wmb_ref, o_ref):
    x = x_ref[...]
    g_sb = jax.nn.sigmoid(_dot(x, wg1_ref[...]) + b1_ref[...])
    g_mb = jax.nn.sigmoid(_dot(x, wg2_ref[...]) + b2_ref[...])
    y_sb = _dot(osb_ref[...], wsb_ref[...])
    y_mb = _dot(omb_ref[...], wmb_ref[...])
    o_ref[...] = (g_sb * y_sb + g_mb * y_mb).astype(o_ref.dtype)


def _mix(x_bf, o_sb, o_mb, wg_bf, b_gate, wsb_bf, wmb_bf, tm=1024, tn=512):
    t, d = x_bf.shape
    w = o_sb.shape[1]
    nj = d // tn
    b2d = b_gate.reshape(1, 2 * d)
    return pl.pallas_call(
        _mix_kernel,
        grid=(t // tm, nj),
        in_specs=[pl.BlockSpec((tm, d), lambda i, j: (i, 0)),
                  pl.BlockSpec((tm, w), lambda i, j: (i, 0)),
                  pl.BlockSpec((tm, w), lambda i, j: (i, 0)),
                  pl.BlockSpec((d, tn), lambda i, j: (0, j)),
                  pl.BlockSpec((d, tn), lambda i, j: (0, nj + j)),
                  pl.BlockSpec((1, tn), lambda i, j: (0, j)),
                  pl.BlockSpec((1, tn), lambda i, j: (0, nj + j)),
                  pl.BlockSpec((w, tn), lambda i, j: (0, j)),
                  pl.BlockSpec((w, tn), lambda i, j: (0, j))],
        out_specs=pl.BlockSpec((tm, tn), lambda i, j: (i, j)),
        out_shape=jax.ShapeDtypeStruct((t, d), _BF16),
        compiler_params=_params(("parallel", "arbitrary")),
        name="mix",
    )(x_bf, o_sb, o_mb, wg_bf, wg_bf, b2d, b2d, wsb_bf, wmb_bf)


def _layer_norm(h, g, b):
    mu = jnp.mean(h, axis=-1, keepdims=True)
    xc = h - mu
    var = jnp.mean(xc * xc, axis=-1, keepdims=True)
    return xc * lax.rsqrt(var + LN_EPS) * g + b


def _outln_kernel(m_ref, w_ref, x_ref, g_ref, b_ref, o_ref, obf_ref):
    h = DN_ALPHA * x_ref[...] + _dot(m_ref[...], w_ref[...])
    y = _layer_norm(h, g_ref[...], b_ref[...])
    o_ref[...] = y
    obf_ref[...] = y.astype(_BF16)


def _outln(mix_bf, w_bf, x2, g, b, tm=256):
    t, d = x2.shape
    row = pl.BlockSpec((tm, d), lambda i: (i, 0))
    vec = pl.BlockSpec((1, d), lambda i: (0, 0))
    return pl.pallas_call(
        _outln_kernel,
        grid=(t // tm,),
        in_specs=[row, pl.BlockSpec((d, d), lambda i: (0, 0)), row, vec, vec],
        out_specs=[row, row],
        out_shape=[jax.ShapeDtypeStruct((t, d), _F32), jax.ShapeDtypeStruct((t, d), _BF16)],
        compiler_params=_params(("parallel",)),
        name="outln",
    )(mix_bf, w_bf, x2, g.reshape(1, d), b.reshape(1, d))


def _pkeys_kernel(x_ref, w_ref, sk_ref, idx_ref, g_ref):
    tm = x_ref.shape[0]
    k = PEER_TOPK
    nk = PEER_NKEYS
    half = PEER_QDIM // 2
    inf = jnp.inf
    pq = _dot(x_ref[...], w_ref[...]).astype(_BF16)
    key_id = lax.broadcasted_iota(jnp.int32, (nk, tm), 0).astype(_F32)
    rank = lax.broadcasted_iota(jnp.int32, (k, tm), 0)

    def first_max(pieces, ids):
        m = pieces[0]
        for piece in pieces[1:]:
            m = jnp.maximum(m, piece)
        m = jnp.max(m, axis=0, keepdims=True)
        f = None
        for piece, pid in zip(pieces, ids):
            c = jnp.where(piece == m, pid, 1e9)
            f = c if f is None else jnp.minimum(f, c)
        return m, jnp.min(f, axis=0, keepdims=True)

    top_val, top_key = [], []
    for p in range(2):
        s = _nt_dot(sk_ref[p].astype(_BF16), pq[:, p * half:(p + 1) * half])
        val = jnp.zeros((k, tm), _F32)
        key = jnp.zeros((k, tm), _F32)
        for r in range(k):
            m, f = first_max([s], [key_id])
            val = jnp.where(rank == r, m, val)
            key = jnp.where(rank == r, f, key)
            s = jnp.where(key_id == f, -inf, s)
        top_val.append(val)
        top_key.append(key)

    rank_f = rank.astype(_F32)
    cand = [top_val[0][a:a + 1] + top_val[1] for a in range(k)]
    expert = [top_key[0][a:a + 1] * float(nk) + top_key[1] for a in range(k)]
    flat = [rank_f + float(a * k) for a in range(k)]
    best = jnp.zeros((k, tm), _F32)
    ids = jnp.zeros((k, tm), _F32)
    for r in range(k):
        m, f = first_max(cand, flat)
        e = None
        for a in range(k):
            hit = flat[a] == f
            ea = jnp.where(hit, expert[a], -1.0)
            e = ea if e is None else jnp.maximum(e, ea)
            cand[a] = jnp.where(hit, -inf, cand[a])
        best = jnp.where(rank == r, m, best)
        ids = jnp.where(rank == r, jnp.max(e, axis=0, keepdims=True), ids)
    w = jnp.exp(best - best[0:1])
    idx_ref[...] = ids.astype(jnp.int32)
    g_ref[...] = w / jnp.sum(w, axis=0, keepdims=True)


def _pkeys(x1_bf, wq_bf, sub_keys, tm=256):
    t, d = x1_bf.shape
    out = pl.BlockSpec((PEER_TOPK, tm), lambda i, h: (h, i))
    return pl.pallas_call(
        _pkeys_kernel,
        grid=(t // tm, PEER_HEADS),
        in_specs=[pl.BlockSpec((tm, d), lambda i, h: (i, 0)),
                  pl.BlockSpec((d, PEER_QDIM), lambda i, h: (0, h)),
                  pl.BlockSpec((2, PEER_NKEYS, PEER_QDIM // 2), lambda i, h: (0, 0, 0))],
        out_specs=[out, out],
        out_shape=[jax.ShapeDtypeStruct((PEER_HEADS * PEER_TOPK, t), jnp.int32),
                   jax.ShapeDtypeStruct((PEER_HEADS * PEER_TOPK, t), _F32)],
        compiler_params=_params(("parallel", "arbitrary")),
        name="pkeys",
    )(x1_bf, wq_bf, sub_keys)


def _peer_kernel(idx_cur, idx_nxt, g_ref, x_ref, xp_ref, uv_hbm, grp_ref, grpt_ref, diag_ref, lg_ref, lb_ref,
                 o_ref, buf0, buf1, buf2, wl0, wl1, wl2, arow, sem):
    nexp = PEER_HEADS * PEER_TOPK
    sub = uv_hbm.shape[1] // 2
    nchunk = nexp // PEER_CHUNK
    s = pl.program_id(0)
    last = pl.num_programs(0) - 1
    bufs = (buf0, buf1, buf2)
    wls = (wl0, wl1, wl2)

    def start(slot, idx_ref, t, e):
        src = uv_hbm.at[idx_ref[t * nexp + e]]
        pltpu.make_async_copy(src, bufs[slot].at[t * nexp + e], sem.at[slot, t]).start(priority=e % 2)

    def wait(slot, t):
        pltpu.make_async_copy(uv_hbm.at[pl.ds(0, nexp)], bufs[slot].at[pl.ds(t * nexp, nexp)],
                              sem.at[slot, t]).wait()

    @pl.when(s == 0)
    def _():
        wl2[...] = jnp.zeros_like(wl2)

        def fill(t, carry):
            buf2[pl.ds(t * nexp, nexp)] = jnp.zeros((nexp,) + buf2.shape[1:], buf2.dtype)
            for e in range(nexp):
                start(0, idx_cur, t, e)
            return carry

        lax.fori_loop(0, PEER_TOK, fill, 0)

    def step(cur):
        nxt = (cur + 1) % 3
        prv = (cur + 2) % 3
        diag = diag_ref[...]
        rows = []
        pending = []

        def flush(keep):
            while len(pending) > keep:
                t0, c0, val = pending.pop(0)
                arow[t0:t0 + 1, c0 * PEER_CHUNK * sub:(c0 + 1) * PEER_CHUNK * sub] = val

        todo = [(t, e) for t in range(PEER_TOK) for e in range(nexp)]

        def issue(count):
            for t, e in todo[:count]:
                start(nxt, idx_nxt, t, e)
            del todo[:count]

        for t in range(PEER_TOK):
            wait(cur, t)
        for t in range(PEER_TOK):
            xb = x_ref[t].astype(_BF16)
            for c in range(nchunk):
                base = t * nexp + c * PEER_CHUNK
                flush(PEER_STORE_LAG)
                issue(PEER_ISSUE[0])
                u = bufs[cur][base:base + PEER_CHUNK, 0:sub, :].reshape(PEER_CHUNK * sub, LANES)
                part = _nt_dot(xb, u) * diag
                pending.append((t, c, jnp.sum(part, axis=0, keepdims=True)))

        flush(0)
        issue(PEER_ISSUE[1])
        a = arow[...]
        a_hi = a.astype(_BF16)
        a_lo = (a - a_hi.astype(_F32)).astype(_BF16)
        act = _dot(a_hi, grp_ref[...]) + _dot(a_lo, grp_ref[...])
        gelu = 0.5 * act * (1.0 + lax.erf(act * (2.0 ** -0.5)))
        wls[cur][...] = _dot((g_ref[...] * gelu).astype(_BF16), grpt_ref[...])

        per_chunk = len(todo) // (PEER_TOK * nchunk)
        for t in range(PEER_TOK):
            total = jnp.zeros((sub, LANES), _F32)
            for c in range(nchunk):
                base = t * nexp + c * PEER_CHUNK
                cols = slice(c * PEER_CHUNK * sub, (c + 1) * PEER_CHUNK * sub)
                issue(per_chunk if (t, c) != (PEER_TOK - 1, nchunk - 1) else len(todo))
                v = bufs[prv][base:base + PEER_CHUNK, sub:2 * sub, :].reshape(PEER_CHUNK * sub, LANES)
                total = total + _dot((wls[prv][t:t + 1, cols] * diag).astype(_BF16), v)
            rows.append(DN_ALPHA * xp_ref[t] + total)
        h = jnp.stack(rows)
        inv_d = 1.0 / (sub * LANES)
        mu = jnp.sum(jnp.sum(h, axis=2, keepdims=True), axis=1, keepdims=True) * inv_d
        xc = h - mu
        var = jnp.sum(jnp.sum(xc * xc, axis=2, keepdims=True), axis=1, keepdims=True) * inv_d
        o_ref[...] = xc * lax.rsqrt(var + LN_EPS) * lg_ref[...] + lb_ref[...]

        @pl.when(s == last)
        def _():
            for t in range(PEER_TOK):
                wait(nxt, t)

    for cur in range(3):
        pl.when(lax.rem(s, 3) == cur)(functools.partial(step, cur))


def _peer(idx_flat, g, x1r, uv, ln_g, ln_b):
    t, sub, _ = x1r.shape
    nexp = PEER_HEADS * PEER_TOPK
    n = t // PEER_TOK
    pos = np.arange(nexp * sub)
    grp = jnp.asarray(pos[:, None] // sub == np.arange(nexp)[None, :], _BF16)
    diag = jnp.asarray(np.arange(sub)[:, None] == np.arange(PEER_CHUNK * sub)[None, :] % sub, _F32)
    const2 = lambda shape: pl.BlockSpec(shape, lambda i: (0, 0))
    this = lambda i: jnp.minimum(i, n - 1)
    prev = lambda i: jnp.maximum(i - 1, 0)
    following = lambda i: jnp.minimum(i + 1, n - 1)
    tok3 = lambda at: pl.BlockSpec((PEER_TOK, sub, LANES), lambda i: (at(i), 0, 0))
    ids = lambda at: pl.BlockSpec((PEER_TOK * nexp,), lambda i: (at(i),), memory_space=pltpu.SMEM)
    return pl.pallas_call(
        _peer_kernel,
        grid=(n + 1,),
        in_specs=[ids(this), ids(following),
                  pl.BlockSpec((PEER_TOK, nexp), lambda i: (this(i), 0)),
                  tok3(this), tok3(prev),
                  pl.BlockSpec(memory_space=pl.ANY),
                  const2((nexp * sub, nexp)),
                  const2((nexp, nexp * sub)),
                  const2((sub, PEER_CHUNK * sub)),
                  const2((sub, LANES)),
                  const2((sub, LANES))],
        out_specs=tok3(prev),
        out_shape=jax.ShapeDtypeStruct((t, sub, LANES), _F32),
        scratch_shapes=[pltpu.VMEM((PEER_TOK * nexp, 2 * sub, LANES), _BF16)] * 3
        + [pltpu.VMEM((PEER_TOK, nexp * sub), _F32)] * 4
        + [pltpu.SemaphoreType.DMA((3, PEER_TOK))],
        compiler_params=_params(("arbitrary",)),
        name="peer",
    )(idx_flat, idx_flat, g, x1r, x1r, uv, grp, grp.T, diag, ln_g.reshape(sub, LANES), ln_b.reshape(sub, LANES))


def kernel(x, w_in, w_gate, b_gate, w_branch_sb, w_branch_moba, w_out, rel_bias, ln1_g, ln1_b,
           w_peer_query, peer_sub_keys, peer_u, peer_v, ln2_g, ln2_b):
    b, s, d = x.shape
    t = b * s
    x2 = x.reshape(t, d)
    x_bf = x2.astype(_BF16)
    sb_w = SB_HEADS * HEAD_DIM
    mb_w = MOBA_HEADS * HEAD_DIM

    proj, colmean = _proj(x_bf, w_in.astype(_BF16))
    proj3 = proj.reshape(b, s, proj.shape[1])
    nblk = s // MOBA_BLOCK
    kmean = colmean.reshape(b, nblk, -1)[:, :, 3 * sb_w + mb_w:3 * sb_w + 2 * mb_w]
    kmean_pad = jnp.pad(kmean, ((0, 0), (0, LANES - nblk), (0, 0)))

    row = np.arange(SB_BLOCK)[:, None]
    col = np.arange(2 * SB_BLOCK)[None, :]
    tri = jnp.asarray((row > col) | (col >= SB_BLOCK), _BF16)

    o_sb = _sb_attention(proj3, tri, 0)
    o_mb = _moba_attention(proj3, kmean_pad, rel_bias, 3 * sb_w // HEAD_DIM)
    mix = _mix(x_bf, o_sb.reshape(t, sb_w), o_mb.reshape(t, mb_w), w_gate.astype(_BF16), b_gate,
               w_branch_sb.astype(_BF16), w_branch_moba.astype(_BF16))
    x1, x1_bf = _outln(mix, w_out.astype(_BF16), x2, ln1_g, ln1_b)

    idx_t, g_t = _pkeys(x1_bf, w_peer_query.astype(_BF16), peer_sub_keys)
    sub = d // LANES
    n_exp = peer_u.shape[0]
    uv = jnp.concatenate([peer_u.reshape(n_exp, sub, LANES), peer_v.reshape(n_exp, sub, LANES)],
                         axis=1).astype(_BF16)
    out = _peer(idx_t.T.reshape(-1), g_t.T, x1.reshape(t, sub, LANES), uv, ln2_g, ln2_b)
    return out.reshape(b, s, d)
```

```python
import functools
import math

import numpy as np
import jax
import jax.numpy as jnp
from jax import lax
from jax.experimental import pallas as pl
from jax.experimental.pallas import tpu as pltpu

HEAD_DIM = 128
SB_HEADS = 8
MOBA_HEADS = 8
SB_BLOCK = 128
SB_TQ = 512
SB_GROUP = 1
SB_DEAD_LOG = -110.0
MOBA_BLOCK = 256
MOBA_TOPK = 3
MOBA_GROUP = 4
REL_BUCKETS = 32
REL_MAX_DIST = 1024
BIAS_TILES = 5
PEER_HEADS = 8
PEER_NKEYS = 128
PEER_QDIM = 256
PEER_TOPK = 16
LN_EPS = 1e-5
DN_ALPHA = 2.0 ** 0.25
NEG = -1e30
LANES = 128
PEER_TOK = 8
PEER_CHUNK = 16
PEER_RING = 4
PEER_STORE_LAG = 3
PEER_ISSUE = (8, 64)
VMEM_LIMIT = 56 * 1024 * 1024

_F32 = jnp.float32
_BF16 = jnp.bfloat16


def _nt_dot(a, b):
    return lax.dot_general(a, b, (((1,), (1,)), ((), ())), preferred_element_type=_F32)


def _dot(a, b):
    return jnp.dot(a, b, preferred_element_type=_F32)


def _params(sem):
    return pltpu.CompilerParams(dimension_semantics=sem, vmem_limit_bytes=VMEM_LIMIT)


def _proj_kernel(x_ref, w_ref, o_ref, m_ref):
    acc = _dot(x_ref[...], w_ref[...])
    o_ref[...] = acc.astype(o_ref.dtype)
    tm, tn = acc.shape
    nb = tm // MOBA_BLOCK
    m_ref[...] = jnp.mean(acc.reshape(nb, MOBA_BLOCK, tn), axis=1)[:, None, :]


def _proj(x_bf, w_bf, tm=1024, tn=512):
    t, d = x_bf.shape
    n = w_bf.shape[1]
    nb = tm // MOBA_BLOCK
    return pl.pallas_call(
        _proj_kernel,
        grid=(t // tm, n // tn),
        in_specs=[pl.BlockSpec((tm, d), lambda i, j: (i, 0)),
                  pl.BlockSpec((d, tn), lambda i, j: (0, j))],
        out_specs=[pl.BlockSpec((tm, tn), lambda i, j: (i, j)),
                   pl.BlockSpec((nb, 1, tn), lambda i, j: (i, 0, j))],
        out_shape=[jax.ShapeDtypeStruct((t, n), _BF16),
                   jax.ShapeDtypeStruct((t // MOBA_BLOCK, 1, n), _F32)],
        compiler_params=_params(("parallel", "arbitrary")),
        name="proj",
    )(x_bf, w_bf)


def _sb_kernel(q_ref, k_ref, v_ref, tri_ref, o_ref, c_ref, acc_ref, *, scale):
    i = pl.program_id(2)
    nsub = SB_TQ // SB_BLOCK
    tri = tri_ref[...]
    row = lax.broadcasted_iota(jnp.int32, (SB_TQ, SB_BLOCK), 0)
    col = lax.broadcasted_iota(jnp.int32, (SB_TQ, SB_BLOCK), 1)

    def blocks(g, j_top, causals):
        lanes = slice(g * HEAD_DIM, (g + 1) * HEAD_DIM)
        q = q_ref[0, :, lanes]
        parts = []
        for u, causal in enumerate(causals):
            start = pl.multiple_of((j_top - u) * SB_BLOCK, SB_BLOCK)
            k = k_ref[0, pl.ds(start, SB_BLOCK), lanes]
            z = _nt_dot(q, k) * scale
            lg = -(jnp.maximum(z, 0.0) + jnp.log1p(jnp.exp(-jnp.abs(z))))
            if causal is not None:
                lg = jnp.where(causal, lg, 0.0)
            lg_hi = lg.astype(_BF16)
            lg_lo = (lg - lg_hi.astype(_F32)).astype(_BF16)
            tr = _dot(lg_hi, tri) + _dot(lg_lo, tri)
            parts.append((start, z + lg, tr, causal))
        c = c_ref[g]
        pv = None
        for start, log_beta, tr, causal in parts:
            w = jnp.exp(log_beta + tr[:, :SB_BLOCK] + c)
            if causal is not None:
                w = jnp.where(causal, w, 0.0)
            d = _dot(w.astype(_BF16), v_ref[0, pl.ds(start, SB_BLOCK), lanes])
            pv = d if pv is None else pv + d
            c = c + tr[:, SB_BLOCK:]
        c_ref[g] = c
        acc_ref[g] += pv

    c_ref[...] = jnp.zeros_like(c_ref)
    acc_ref[...] = jnp.zeros_like(acc_ref)

    diag_masks = [col + dj * SB_BLOCK < row for dj in range(nsub - 1, -1, -1)]
    for g in range(SB_GROUP):
        blocks(g, nsub * i + nsub - 1, diag_masks)

    def live(carry):
        jj, log_rest = carry
        return jnp.logical_and(jj < i, log_rest > SB_DEAD_LOG)

    def body(carry):
        jj, _ = carry
        for g in range(SB_GROUP):
            blocks(g, nsub * (i - jj) - 1, [None] * nsub)
        return jj + 1, jnp.max(c_ref[...])

    lax.while_loop(live, body, (jnp.int32(0), jnp.max(c_ref[...])))
    for g in range(SB_GROUP):
        o_ref[0, :, g * HEAD_DIM:(g + 1) * HEAD_DIM] = acc_ref[g].astype(o_ref.dtype)


def _sb_attention(proj3, tri, col0):
    b, s, _ = proj3.shape
    nq = s // SB_TQ
    gw = SB_GROUP * HEAD_DIM
    ng = SB_HEADS // SB_GROUP
    c0 = col0 // SB_GROUP
    return pl.pallas_call(
        functools.partial(_sb_kernel, scale=HEAD_DIM ** -0.5),
        grid=(b, ng, nq),
        in_specs=[pl.BlockSpec((1, SB_TQ, gw), lambda bi, h, i: (bi, i, c0 + h)),
                  pl.BlockSpec((1, s, gw), lambda bi, h, i: (bi, 0, c0 + ng + h)),
                  pl.BlockSpec((1, s, gw), lambda bi, h, i: (bi, 0, c0 + 2 * ng + h)),
                  pl.BlockSpec((SB_BLOCK, 2 * SB_BLOCK), lambda bi, h, i: (0, 0))],
        out_specs=pl.BlockSpec((1, SB_TQ, gw), lambda bi, h, i: (bi, i, h)),
        out_shape=jax.ShapeDtypeStruct((b, s, SB_HEADS * HEAD_DIM), _BF16),
        scratch_shapes=[pltpu.VMEM((SB_GROUP, SB_TQ, HEAD_DIM), _F32),
                        pltpu.VMEM((SB_GROUP, SB_TQ, HEAD_DIM), _F32)],
        compiler_params=_params(("parallel", "parallel", "arbitrary")),
        name="sb_attn",
    )(proj3, proj3, proj3, tri)


def _rel_bucket_table():
    n_exact = REL_BUCKETS // 2
    d = np.arange(BIAS_TILES)[:, None, None]
    q = np.arange(MOBA_BLOCK)[None, :, None]
    k = np.arange(MOBA_BLOCK)[None, None, :]
    rel = np.maximum(d * MOBA_BLOCK + q - k, 0)
    logd = (np.log(np.maximum(rel, 1).astype(np.float32) / np.float32(n_exact))
            / np.float32(math.log(REL_MAX_DIST / n_exact))).astype(np.float32)
    large = n_exact + (logd * np.float32(REL_BUCKETS - n_exact)).astype(np.int32)
    large = np.minimum(large, REL_BUCKETS - 1)
    return np.where(rel < n_exact, rel, large).astype(np.int32)


def _moba_kernel(rb_ref, q_ref, k_ref, v_ref, km_ref, bkt_ref, o_ref, bias_ref, *, scale):
    h = pl.program_id(0)
    first = jnp.logical_and(pl.program_id(1) == 0, pl.program_id(2) == 0)
    i = pl.program_id(2)

    @pl.when(first)
    def _():
        for d in range(BIAS_TILES):
            bkt = bkt_ref[d]
            tile = jnp.zeros((MOBA_BLOCK, MOBA_BLOCK), _F32)
            for bi in range(REL_BUCKETS):
                tile = jnp.where(bkt == bi, rb_ref[h, bi], tile)
            bias_ref[d] = tile
        bias_ref[BIAS_TILES] = jnp.full((MOBA_BLOCK, MOBA_BLOCK), rb_ref[h, REL_BUCKETS - 1], _F32)

    q = q_ref[0]
    lane = lax.broadcasted_iota(jnp.int32, (MOBA_BLOCK, LANES), 1)
    valid = lane < i
    gate = _nt_dot(q, km_ref[0].astype(_BF16))
    g = jnp.where(valid, gate, -jnp.inf)
    sel = jnp.zeros((MOBA_BLOCK, LANES), jnp.bool_)
    for _ in range(MOBA_TOPK):
        m = jnp.max(g, axis=-1, keepdims=True)
        first_idx = jnp.min(jnp.where(g == m, lane, LANES), axis=-1, keepdims=True)
        pick = lane == first_idx
        sel = jnp.logical_or(sel, pick)
        g = jnp.where(pick, -jnp.inf, g)
    sel_f = jnp.where(jnp.logical_and(sel, valid), 1.0, 0.0)

    row = lax.broadcasted_iota(jnp.int32, (MOBA_BLOCK, MOBA_BLOCK), 0)
    col = lax.broadcasted_iota(jnp.int32, (MOBA_BLOCK, MOBA_BLOCK), 1)

    def block(j):
        start = pl.multiple_of(j * MOBA_BLOCK, MOBA_BLOCK)
        return k_ref[0, pl.ds(start, MOBA_BLOCK), :], v_ref[0, pl.ds(start, MOBA_BLOCK), :]

    kb, vb = block(i)
    s = _nt_dot(q, kb) * scale + bias_ref[0]
    s = jnp.where(col <= row, s, NEG)
    m0 = jnp.max(s, axis=-1, keepdims=True)
    p = jnp.exp(s - m0)
    l0 = jnp.sum(p, axis=-1, keepdims=True)
    acc0 = _dot(p.astype(_BF16), vb)

    def body(jg, carry):
        m, l, acc = carry
        scores, values = [], []
        for u in range(MOBA_GROUP):
            j = jg * MOBA_GROUP + u
            kb, vb = block(j)
            d = jnp.clip(i - j, 0, BIAS_TILES)
            s = _nt_dot(q, kb) * scale + bias_ref[d]
            chosen = jnp.sum(jnp.where(lane == j, sel_f, 0.0), axis=-1, keepdims=True)
            scores.append(jnp.where(chosen > 0.5, s, NEG))
            values.append(vb)
        top = functools.reduce(jnp.maximum, scores)
        m_new = jnp.maximum(m, jnp.max(top, axis=-1, keepdims=True))
        a = jnp.exp(m - m_new)
        probs = [jnp.exp(s - m_new) for s in scores]
        l = a * l + jnp.sum(functools.reduce(jnp.add, probs), axis=-1, keepdims=True)
        acc = a * acc + functools.reduce(jnp.add, [_dot(p.astype(_BF16), vb) for p, vb in zip(probs, values)])
        return m_new, l, acc

    ngroups = (i + MOBA_GROUP - 1) // MOBA_GROUP
    m, l, acc = lax.fori_loop(0, ngroups, body, (m0, l0, acc0))
    o_ref[0] = (acc / l).astype(o_ref.dtype)


def _moba_attention(proj3, kmean_pad, rel_bias, col0):
    b, s, _ = proj3.shape
    nq = s // MOBA_BLOCK
    assert nq % MOBA_GROUP == 0 and nq <= LANES
    bkt = jnp.asarray(_rel_bucket_table())
    return pl.pallas_call(
        functools.partial(_moba_kernel, scale=HEAD_DIM ** -0.5),
        grid=(MOBA_HEADS, b, nq),
        in_specs=[pl.BlockSpec(memory_space=pltpu.SMEM),
                  pl.BlockSpec((1, MOBA_BLOCK, HEAD_DIM), lambda h, bi, i: (bi, i, col0 + h)),
                  pl.BlockSpec((1, s, HEAD_DIM), lambda h, bi, i: (bi, 0, col0 + MOBA_HEADS + h)),
                  pl.BlockSpec((1, s, HEAD_DIM), lambda h, bi, i: (bi, 0, col0 + 2 * MOBA_HEADS + h)),
                  pl.BlockSpec((1, LANES, HEAD_DIM), lambda h, bi, i: (bi, 0, h)),
                  pl.BlockSpec((BIAS_TILES, MOBA_BLOCK, MOBA_BLOCK), lambda h, bi, i: (0, 0, 0))],
        out_specs=pl.BlockSpec((1, MOBA_BLOCK, HEAD_DIM), lambda h, bi, i: (bi, i, h)),
        out_shape=jax.ShapeDtypeStruct((b, s, MOBA_HEADS * HEAD_DIM), _BF16),
        scratch_shapes=[pltpu.VMEM((BIAS_TILES + 1, MOBA_BLOCK, MOBA_BLOCK), _F32)],
        compiler_params=_params(("arbitrary", "arbitrary", "arbitrary")),
        name="moba_attn",
    )(rel_bias, proj3, proj3, proj3, kmean_pad, bkt)


def _mix_kernel(x_ref, osb_ref, omb_ref, wg1_ref, wg2_ref, b1_ref, b2_ref, wsb_ref, wmb_ref, o_ref):
    x = x_ref[...]
    g_sb = jax.nn.sigmoid(_dot(x, wg1_ref[...]) + b1_ref[...])
    g_mb = jax.nn.sigmoid(_dot(x, wg2_ref[...]) + b2_ref[...])
    y_sb = _dot(osb_ref[...], wsb_ref[...])
    y_mb = _dot(omb_ref[...], wmb_ref[...])
    o_ref[...] = (g_sb * y_sb + g_mb * y_mb).astype(o_ref.dtype)


def _mix(x_bf, o_sb, o_mb, wg_bf, b_gate, wsb_bf, wmb_bf, tm=1024, tn=512):
    t, d = x_bf.shape
    w = o_sb.shape[1]
    nj = d // tn
    b2d = b_gate.reshape(1, 2 * d)
    return pl.pallas_call(
        _mix_kernel,
        grid=(t // tm, nj),
        in_specs=[pl.BlockSpec((tm, d), lambda i, j: (i, 0)),
                  pl.BlockSpec((tm, w), lambda i, j: (i, 0)),
                  pl.BlockSpec((tm, w), lambda i, j: (i, 0)),
                  pl.BlockSpec((d, tn), lambda i, j: (0, j)),
                  pl.BlockSpec((d, tn), lambda i, j: (0, nj + j)),
                  pl.BlockSpec((1, tn), lambda i, j: (0, j)),
                  pl.BlockSpec((1, tn), lambda i, j: (0, nj + j)),
                  pl.BlockSpec((w, tn), lambda i, j: (0, j)),
                  pl.BlockSpec((w, tn), lambda i, j: (0, j))],
        out_specs=pl.BlockSpec((tm, tn), lambda i, j: (i, j)),
        out_shape=jax.ShapeDtypeStruct((t, d), _BF16),
        compiler_params=_params(("parallel", "arbitrary")),
        name="mix",
    )(x_bf, o_sb, o_mb, wg_bf, wg_bf, b2d, b2d, wsb_bf, wmb_bf)


def _layer_norm(h, g, b):
    mu = jnp.mean(h, axis=-1, keepdims=True)
    xc = h - mu
    var = jnp.mean(xc * xc, axis=-1, keepdims=True)
    return xc * lax.rsqrt(var + LN_EPS) * g + b


def _outln_kernel(m_ref, w_ref, x_ref, g_ref, b_ref, o_ref, obf_ref):
    h = DN_ALPHA * x_ref[...] + _dot(m_ref[...], w_ref[...])
    y = _layer_norm(h, g_ref[...], b_ref[...])
    o_ref[...] = y
    obf_ref[...] = y.astype(_BF16)


def _outln(mix_bf, w_bf, x2, g, b, tm=256):
    t, d = x2.shape
    row = pl.BlockSpec((tm, d), lambda i: (i, 0))
    vec = pl.BlockSpec((1, d), lambda i: (0, 0))
    return pl.pallas_call(
        _outln_kernel,
        grid=(t // tm,),
        in_specs=[row, pl.BlockSpec((d, d), lambda i: (0, 0)), row, vec, vec],
        out_specs=[row, row],
        out_shape=[jax.ShapeDtypeStruct((t, d), _F32), jax.ShapeDtypeStruct((t, d), _BF16)],
        compiler_params=_params(("parallel",)),
        name="outln",
    )(mix_bf, w_bf, x2, g.reshape(1, d), b.reshape(1, d))


def _pkeys_kernel(x_ref, w_ref, sk_ref, idx_ref, g_ref):
    tm = x_ref.shape[0]
    k = PEER_TOPK
    nk = PEER_NKEYS
    half = PEER_QDIM // 2
    inf = jnp.inf
    pq = _dot(x_ref[...], w_ref[...]).astype(_BF16)
    key_id = lax.broadcasted_iota(jnp.int32, (nk, tm), 0).astype(_F32)
    rank = lax.broadcasted_iota(jnp.int32, (k, tm), 0)

    def first_max(pieces, ids):
        m = pieces[0]
        for piece in pieces[1:]:
            m = jnp.maximum(m, piece)
        m = jnp.max(m, axis=0, keepdims=True)
        f = None
        for piece, pid in zip(pieces, ids):
            c = jnp.where(piece == m, pid, 1e9)
            f = c if f is None else jnp.minimum(f, c)
        return m, jnp.min(f, axis=0, keepdims=True)

    top_val, top_key = [], []
    for p in range(2):
        s = _nt_dot(sk_ref[p].astype(_BF16), pq[:, p * half:(p + 1) * half])
        val = jnp.zeros((k, tm), _F32)
        key = jnp.zeros((k, tm), _F32)
        for r in range(k):
            m, f = first_max([s], [key_id])
            val = jnp.where(rank == r, m, val)
            key = jnp.where(rank == r, f, key)
            s = jnp.where(key_id == f, -inf, s)
        top_val.append(val)
        top_key.append(key)

    rank_f = rank.astype(_F32)
    cand = [top_val[0][a:a + 1] + top_val[1] for a in range(k)]
    expert = [top_key[0][a:a + 1] * float(nk) + top_key[1] for a in range(k)]
    flat = [rank_f + float(a * k) for a in range(k)]
    best = jnp.zeros((k, tm), _F32)
    ids = jnp.zeros((k, tm), _F32)
    for r in range(k):
        m, f = first_max(cand, flat)
        e = None
        for a in range(k):
            hit = flat[a] == f
            ea = jnp.where(hit, expert[a], -1.0)
            e = ea if e is None else jnp.maximum(e, ea)
            cand[a] = jnp.where(hit, -inf, cand[a])
        best = jnp.where(rank == r, m, best)
        ids = jnp.where(rank == r, jnp.max(e, axis=0, keepdims=True), ids)
    w = jnp.exp(best - best[0:1])
    idx_ref[...] = ids.astype(jnp.int32)
    g_ref[...] = w / jnp.sum(w, axis=0, keepdims=True)


def _pkeys(x1_bf, wq_bf, sub_keys, tm=256):
    t, d = x1_bf.shape
    out = pl.BlockSpec((PEER_TOPK, tm), lambda i, h: (h, i))
    return pl.pallas_call(
        _pkeys_kernel,
        grid=(t // tm, PEER_HEADS),
        in_specs=[pl.BlockSpec((tm, d), lambda i, h: (i, 0)),
                  pl.BlockSpec((d, PEER_QDIM), lambda i, h: (0, h)),
                  pl.BlockSpec((2, PEER_NKEYS, PEER_QDIM // 2), lambda i, h: (0, 0, 0))],
        out_specs=[out, out],
        out_shape=[jax.ShapeDtypeStruct((PEER_HEADS * PEER_TOPK, t), jnp.int32),
                   jax.ShapeDtypeStruct((PEER_HEADS * PEER_TOPK, t), _F32)],
        compiler_params=_params(("parallel", "arbitrary")),
        name="pkeys",
    )(x1_bf, wq_bf, sub_keys)


def _peer_kernel(idx_cur, idx_one, idx_two, g_ref, x_ref, xp_ref, uv_hbm, grp_ref, grpt_ref, diag_ref,
                 lg_ref, lb_ref, o_ref, buf0, buf1, buf2, buf3, wl0, wl1, wl2, wl3, arow, sem):
    nexp = PEER_HEADS * PEER_TOPK
    sub = uv_hbm.shape[1] // 2
    nchunk = nexp // PEER_CHUNK
    s = pl.program_id(0)
    last = pl.num_programs(0) - 1
    bufs = (buf0, buf1, buf2, buf3)
    wls = (wl0, wl1, wl2, wl3)

    def start(slot, idx_ref, t, e):
        src = uv_hbm.at[idx_ref[t * nexp + e]]
        pltpu.make_async_copy(src, bufs[slot].at[t * nexp + e], sem.at[slot, t]).start(priority=e % 2)

    def wait(slot, t):
        pltpu.make_async_copy(uv_hbm.at[pl.ds(0, nexp)], bufs[slot].at[pl.ds(t * nexp, nexp)],
                              sem.at[slot, t]).wait()

    @pl.when(s == 0)
    def _():
        wl3[...] = jnp.zeros_like(wl3)

        def fill(t, carry):
            buf3[pl.ds(t * nexp, nexp)] = jnp.zeros((nexp,) + buf3.shape[1:], buf3.dtype)
            for e in range(nexp):
                start(0, idx_cur, t, e)
            for e in range(nexp):
                start(1, idx_one, t, e)
            return carry

        lax.fori_loop(0, PEER_TOK, fill, 0)

    def step(cur):
        nxt = (cur + 2) % PEER_RING
        prv = (cur + PEER_RING - 1) % PEER_RING
        diag = diag_ref[...]
        rows = []
        pending = []

        def flush(keep):
            while len(pending) > keep:
                t0, c0, val = pending.pop(0)
                arow[t0:t0 + 1, c0 * PEER_CHUNK * sub:(c0 + 1) * PEER_CHUNK * sub] = val

        todo = [(t, e) for t in range(PEER_TOK) for e in range(nexp)]

        def issue(count):
            for t, e in todo[:count]:
                start(nxt, idx_two, t, e)
            del todo[:count]

        for t in range(PEER_TOK):
            wait(cur, t)
        for t in range(PEER_TOK):
            xb = x_ref[t].astype(_BF16)
            for c in range(nchunk):
                base = t * nexp + c * PEER_CHUNK
                flush(PEER_STORE_LAG)
                issue(PEER_ISSUE[0])
                u = bufs[cur][base:base + PEER_CHUNK, 0:sub, :].reshape(PEER_CHUNK * sub, LANES)
                part = _nt_dot(xb, u) * diag
                pending.append((t, c, jnp.sum(part, axis=0, keepdims=True)))

        flush(0)
        issue(PEER_ISSUE[1])
        a = arow[...]
        a_hi = a.astype(_BF16)
        a_lo = (a - a_hi.astype(_F32)).astype(_BF16)
        act = _dot(a_hi, grp_ref[...]) + _dot(a_lo, grp_ref[...])
        gelu = 0.5 * act * (1.0 + lax.erf(act * (2.0 ** -0.5)))
        wls[cur][...] = _dot((g_ref[...] * gelu).astype(_BF16), grpt_ref[...])

        per_chunk = len(todo) // (PEER_TOK * nchunk)
        for t in range(PEER_TOK):
            total = jnp.zeros((sub, LANES), _F32)
            for c in range(nchunk):
                base = t * nexp + c * PEER_CHUNK
                cols = slice(c * PEER_CHUNK * sub, (c + 1) * PEER_CHUNK * sub)
                issue(per_chunk if (t, c) != (PEER_TOK - 1, nchunk - 1) else len(todo))
                v = bufs[prv][base:base + PEER_CHUNK, sub:2 * sub, :].reshape(PEER_CHUNK * sub, LANES)
                total = total + _dot((wls[prv][t:t + 1, cols] * diag).astype(_BF16), v)
            rows.append(DN_ALPHA * xp_ref[t] + total)
        h = jnp.stack(rows)
        inv_d = 1.0 / (sub * LANES)
        mu = jnp.sum(jnp.sum(h, axis=2, keepdims=True), axis=1, keepdims=True) * inv_d
        xc = h - mu
        var = jnp.sum(jnp.sum(xc * xc, axis=2, keepdims=True), axis=1, keepdims=True) * inv_d
        o_ref[...] = xc * lax.rsqrt(var + LN_EPS) * lg_ref[...] + lb_ref[...]

        @pl.when(s == last)
        def _():
            for slot in ((cur + 1) % PEER_RING, nxt):
                for t in range(PEER_TOK):
                    wait(slot, t)

    for cur in range(PEER_RING):
        pl.when(lax.rem(s, PEER_RING) == cur)(functools.partial(step, cur))


def _peer(idx_flat, g, x1r, uv, ln_g, ln_b):
    t, sub, _ = x1r.shape
    nexp = PEER_HEADS * PEER_TOPK
    n = t // PEER_TOK
    pos = np.arange(nexp * sub)
    grp = jnp.asarray(pos[:, None] // sub == np.arange(nexp)[None, :], _BF16)
    diag = jnp.asarray(np.arange(sub)[:, None] == np.arange(PEER_CHUNK * sub)[None, :] % sub, _F32)
    const2 = lambda shape: pl.BlockSpec(shape, lambda i: (0, 0))
    this = lambda i: jnp.minimum(i, n - 1)
    prev = lambda i: jnp.maximum(i - 1, 0)
    ahead = lambda k: (lambda i: jnp.minimum(i + k, n - 1))
    tok3 = lambda at: pl.BlockSpec((PEER_TOK, sub, LANES), lambda i: (at(i), 0, 0))
    ids = lambda at: pl.BlockSpec((PEER_TOK * nexp,), lambda i: (at(i),), memory_space=pltpu.SMEM)
    return pl.pallas_call(
        _peer_kernel,
        grid=(n + 1,),
        in_specs=[ids(this), ids(ahead(1)), ids(ahead(2)),
                  pl.BlockSpec((PEER_TOK, nexp), lambda i: (this(i), 0)),
                  tok3(this), tok3(prev),
                  pl.BlockSpec(memory_space=pl.ANY),
                  const2((nexp * sub, nexp)),
                  const2((nexp, nexp * sub)),
                  const2((sub, PEER_CHUNK * sub)),
                  const2((sub, LANES)),
                  const2((sub, LANES))],
        out_specs=tok3(prev),
        out_shape=jax.ShapeDtypeStruct((t, sub, LANES), _F32),
        scratch_shapes=[pltpu.VMEM((PEER_TOK * nexp, 2 * sub, LANES), _BF16)] * PEER_RING
        + [pltpu.VMEM((PEER_TOK, nexp * sub), _F32)] * (PEER_RING + 1)
        + [pltpu.SemaphoreType.DMA((PEER_RING, PEER_TOK))],
        compiler_params=_params(("arbitrary",)),
        name="peer",
    )(idx_flat, idx_flat, idx_flat, g, x1r, x1r, uv, grp, grp.T, diag,
      ln_g.reshape(sub, LANES), ln_b.reshape(sub, LANES))


def kernel(x, w_in, w_gate, b_gate, w_branch_sb, w_branch_moba, w_out, rel_bias, ln1_g, ln1_b,
           w_peer_query, peer_sub_keys, peer_u, peer_v, ln2_g, ln2_b):
    b, s, d = x.shape
    t = b * s
    x2 = x.reshape(t, d)
    x_bf = x2.astype(_BF16)
    sb_w = SB_HEADS * HEAD_DIM
    mb_w = MOBA_HEADS * HEAD_DIM

    proj, colmean = _proj(x_bf, w_in.astype(_BF16))
    proj3 = proj.reshape(b, s, proj.shape[1])
    nblk = s // MOBA_BLOCK
    kmean = colmean.reshape(b, nblk, -1)[:, :, 3 * sb_w + mb_w:3 * sb_w + 2 * mb_w]
    kmean_pad = jnp.pad(kmean, ((0, 0), (0, LANES - nblk), (0, 0)))

    row = np.arange(SB_BLOCK)[:, None]
    col = np.arange(2 * SB_BLOCK)[None, :]
    tri = jnp.asarray((row > col) | (col >= SB_BLOCK), _BF16)

    o_sb = _sb_attention(proj3, tri, 0)
    o_mb = _moba_attention(proj3, kmean_pad, rel_bias, 3 * sb_w // HEAD_DIM)
    mix = _mix(x_bf, o_sb.reshape(t, sb_w), o_mb.reshape(t, mb_w), w_gate.astype(_BF16), b_gate,
               w_branch_sb.astype(_BF16), w_branch_moba.astype(_BF16))
    x1, x1_bf = _outln(mix, w_out.astype(_BF16), x2, ln1_g, ln1_b)

    idx_t, g_t = _pkeys(x1_bf, w_peer_query.astype(_BF16), peer_sub_keys)
    sub = d // LANES
    n_exp = peer_u.shape[0]
    uv = jnp.concatenate([peer_u.reshape(n_exp, sub, LANES), peer_v.reshape(n_exp, sub, LANES)],
                         axis=1).astype(_BF16)
    out = _peer(idx_t.T.reshape(-1), g_t.T, x1.reshape(t, sub, LANES), uv, ln2_g, ln2_b)
    return out.reshape(b, s, d)
```

```python
import functools
import math

import numpy as np
import jax
import jax.numpy as jnp
from jax import lax
from jax.experimental import pallas as pl
from jax.experimental.pallas import tpu as pltpu

HEAD_DIM = 128
SB_HEADS = 8
MOBA_HEADS = 8
SB_BLOCK = 128
SB_TQ = 512
SB_GROUP = 1
SB_DEAD_LOG = -110.0
MOBA_BLOCK = 256
MOBA_TOPK = 3
MOBA_GROUP = 4
REL_BUCKETS = 32
REL_MAX_DIST = 1024
BIAS_TILES = 5
PEER_HEADS = 8
PEER_NKEYS = 128
PEER_QDIM = 256
PEER_TOPK = 16
LN_EPS = 1e-5
DN_ALPHA = 2.0 ** 0.25
NEG = -1e30
LANES = 128
PEER_TOK = 8
PEER_CHUNK = 16
PEER_RING = 4
PEER_STORE_LAG = 3
PEER_ISSUE = (8, 64)
VMEM_LIMIT = 56 * 1024 * 1024

_F32 = jnp.float32
_BF16 = jnp.bfloat16


def _nt_dot(a, b):
    return lax.dot_general(a, b, (((1,), (1,)), ((), ())), preferred_element_type=_F32)


def _dot(a, b):
    return jnp.dot(a, b, preferred_element_type=_F32)


def _params(sem):
    return pltpu.CompilerParams(dimension_semantics=sem, vmem_limit_bytes=VMEM_LIMIT)


def _proj_kernel(x_ref, w_ref, o_ref, m_ref):
    acc = _dot(x_ref[...], w_ref[...])
    o_ref[...] = acc.astype(o_ref.dtype)
    tm, tn = acc.shape
    nb = tm // MOBA_BLOCK
    m_ref[...] = jnp.mean(acc.reshape(nb, MOBA_BLOCK, tn), axis=1)[:, None, :]


def _proj(x_bf, w_bf, tm=1024, tn=512):
    t, d = x_bf.shape
    n = w_bf.shape[1]
    nb = tm // MOBA_BLOCK
    return pl.pallas_call(
        _proj_kernel,
        grid=(t // tm, n // tn),
        in_specs=[pl.BlockSpec((tm, d), lambda i, j: (i, 0)),
                  pl.BlockSpec((d, tn), lambda i, j: (0, j))],
        out_specs=[pl.BlockSpec((tm, tn), lambda i, j: (i, j)),
                   pl.BlockSpec((nb, 1, tn), lambda i, j: (i, 0, j))],
        out_shape=[jax.ShapeDtypeStruct((t, n), _BF16),
                   jax.ShapeDtypeStruct((t // MOBA_BLOCK, 1, n), _F32)],
        compiler_params=_params(("parallel", "arbitrary")),
        name="proj",
    )(x_bf, w_bf)


def _sb_kernel(q_ref, k_ref, v_ref, tri_ref, o_ref, c_ref, acc_ref, *, scale):
    i = pl.program_id(2)
    nsub = SB_TQ // SB_BLOCK
    tri = tri_ref[...]
    row = lax.broadcasted_iota(jnp.int32, (SB_TQ, SB_BLOCK), 0)
    col = lax.broadcasted_iota(jnp.int32, (SB_TQ, SB_BLOCK), 1)

    def blocks(g, j_top, causals):
        lanes = slice(g * HEAD_DIM, (g + 1) * HEAD_DIM)
        q = q_ref[0, :, lanes]
        parts = []
        for u, causal in enumerate(causals):
            start = pl.multiple_of((j_top - u) * SB_BLOCK, SB_BLOCK)
            k = k_ref[0, pl.ds(start, SB_BLOCK), lanes]
            z = _nt_dot(q, k) * scale
            lg = -(jnp.maximum(z, 0.0) + jnp.log1p(jnp.exp(-jnp.abs(z))))
            if causal is not None:
                lg = jnp.where(causal, lg, 0.0)
            lg_hi = lg.astype(_BF16)
            lg_lo = (lg - lg_hi.astype(_F32)).astype(_BF16)
            tr = _dot(lg_hi, tri) + _dot(lg_lo, tri)
            parts.append((start, z + lg, tr, causal))
        c = c_ref[g]
        pv = None
        for start, log_beta, tr, causal in parts:
            w = jnp.exp(log_beta + tr[:, :SB_BLOCK] + c)
            if causal is not None:
                w = jnp.where(causal, w, 0.0)
            d = _dot(w.astype(_BF16), v_ref[0, pl.ds(start, SB_BLOCK), lanes])
            pv = d if pv is None else pv + d
            c = c + tr[:, SB_BLOCK:]
        c_ref[g] = c
        acc_ref[g] += pv

    c_ref[...] = jnp.zeros_like(c_ref)
    acc_ref[...] = jnp.zeros_like(acc_ref)

    diag_masks = [col + dj * SB_BLOCK < row for dj in range(nsub - 1, -1, -1)]
    for g in range(SB_GROUP):
        blocks(g, nsub * i + nsub - 1, diag_masks)

    def live(carry):
        jj, log_rest = carry
        return jnp.logical_and(jj < i, log_rest > SB_DEAD_LOG)

    def body(carry):
        jj, _ = carry
        for g in range(SB_GROUP):
            blocks(g, nsub * (i - jj) - 1, [None] * nsub)
        return jj + 1, jnp.max(c_ref[...])

    lax.while_loop(live, body, (jnp.int32(0), jnp.max(c_ref[...])))
    for g in range(SB_GROUP):
        o_ref[0, :, g * HEAD_DIM:(g + 1) * HEAD_DIM] = acc_ref[g].astype(o_ref.dtype)


def _sb_attention(proj3, tri, col0):
    b, s, _ = proj3.shape
    nq = s // SB_TQ
    gw = SB_GROUP * HEAD_DIM
    ng = SB_HEADS // SB_GROUP
    c0 = col0 // SB_GROUP
    return pl.pallas_call(
        functools.partial(_sb_kernel, scale=HEAD_DIM ** -0.5),
        grid=(b, ng, nq),
        in_specs=[pl.BlockSpec((1, SB_TQ, gw), lambda bi, h, i: (bi, i, c0 + h)),
                  pl.BlockSpec((1, s, gw), lambda bi, h, i: (bi, 0, c0 + ng + h)),
                  pl.BlockSpec((1, s, gw), lambda bi, h, i: (bi, 0, c0 + 2 * ng + h)),
                  pl.BlockSpec((SB_BLOCK, 2 * SB_BLOCK), lambda bi, h, i: (0, 0))],
        out_specs=pl.BlockSpec((1, SB_TQ, gw), lambda bi, h, i: (bi, i, h)),
        out_shape=jax.ShapeDtypeStruct((b, s, SB_HEADS * HEAD_DIM), _BF16),
        scratch_shapes=[pltpu.VMEM((SB_GROUP, SB_TQ, HEAD_DIM), _F32),
                        pltpu.VMEM((SB_GROUP, SB_TQ, HEAD_DIM), _F32)],
        compiler_params=_params(("parallel", "parallel", "arbitrary")),
        name="sb_attn",
    )(proj3, proj3, proj3, tri)


def _rel_bucket_table():
    n_exact = REL_BUCKETS // 2
    d = np.arange(BIAS_TILES)[:, None, None]
    q = np.arange(MOBA_BLOCK)[None, :, None]
    k = np.arange(MOBA_BLOCK)[None, None, :]
    rel = np.maximum(d * MOBA_BLOCK + q - k, 0)
    logd = (np.log(np.maximum(rel, 1).astype(np.float32) / np.float32(n_exact))
            / np.float32(math.log(REL_MAX_DIST / n_exact))).astype(np.float32)
    large = n_exact + (logd * np.float32(REL_BUCKETS - n_exact)).astype(np.int32)
    large = np.minimum(large, REL_BUCKETS - 1)
    return np.where(rel < n_exact, rel, large).astype(np.int32)


def _moba_kernel(rb_ref, q_ref, k_ref, v_ref, km_ref, bkt_ref, o_ref, bias_ref, *, scale):
    h = pl.program_id(0)
    first = jnp.logical_and(pl.program_id(1) == 0, pl.program_id(2) == 0)
    i = pl.program_id(2)

    @pl.when(first)
    def _():
        for d in range(BIAS_TILES):
            bkt = bkt_ref[d]
            tile = jnp.zeros((MOBA_BLOCK, MOBA_BLOCK), _F32)
            for bi in range(REL_BUCKETS):
                tile = jnp.where(bkt == bi, rb_ref[h, bi], tile)
            bias_ref[d] = tile
        bias_ref[BIAS_TILES] = jnp.full((MOBA_BLOCK, MOBA_BLOCK), rb_ref[h, REL_BUCKETS - 1], _F32)

    q = q_ref[0]
    lane = lax.broadcasted_iota(jnp.int32, (MOBA_BLOCK, LANES), 1)
    valid = lane < i
    gate = _nt_dot(q, km_ref[0].astype(_BF16))
    g = jnp.where(valid, gate, -jnp.inf)
    sel = jnp.zeros((MOBA_BLOCK, LANES), jnp.bool_)
    for _ in range(MOBA_TOPK):
        m = jnp.max(g, axis=-1, keepdims=True)
        first_idx = jnp.min(jnp.where(g == m, lane, LANES), axis=-1, keepdims=True)
        pick = lane == first_idx
        sel = jnp.logical_or(sel, pick)
        g = jnp.where(pick, -jnp.inf, g)
    sel_f = jnp.where(jnp.logical_and(sel, valid), 1.0, 0.0)

    row = lax.broadcasted_iota(jnp.int32, (MOBA_BLOCK, MOBA_BLOCK), 0)
    col = lax.broadcasted_iota(jnp.int32, (MOBA_BLOCK, MOBA_BLOCK), 1)

    def block(j):
        start = pl.multiple_of(j * MOBA_BLOCK, MOBA_BLOCK)
        return k_ref[0, pl.ds(start, MOBA_BLOCK), :], v_ref[0, pl.ds(start, MOBA_BLOCK), :]

    def past_group(jg):
        scores, values = [], []
        for u in range(MOBA_GROUP):
            j = jg * MOBA_GROUP + u
            kb, vb = block(j)
            d = jnp.clip(i - j, 0, BIAS_TILES)
            s = _nt_dot(q, kb) * scale + bias_ref[d]
            chosen = jnp.sum(jnp.where(lane == j, sel_f, 0.0), axis=-1, keepdims=True)
            scores.append(jnp.where(chosen > 0.5, s, NEG))
            values.append(vb)
        return scores, values

    def fold(carry, scores, values):
        m, l, acc = carry
        top = functools.reduce(jnp.maximum, scores)
        m_new = jnp.maximum(m, jnp.max(top, axis=-1, keepdims=True))
        a = jnp.exp(m - m_new)
        probs = [jnp.exp(s - m_new) for s in scores]
        l = a * l + jnp.sum(functools.reduce(jnp.add, probs), axis=-1, keepdims=True)
        acc = a * acc + functools.reduce(jnp.add, [_dot(p.astype(_BF16), vb) for p, vb in zip(probs, values)])
        return m_new, l, acc

    kb, vb = block(i)
    own = jnp.where(col <= row, _nt_dot(q, kb) * scale + bias_ref[0], NEG)
    scores, values = past_group(0)
    init = (jnp.full((MOBA_BLOCK, 1), NEG, _F32), jnp.zeros((MOBA_BLOCK, 1), _F32),
            jnp.zeros((MOBA_BLOCK, HEAD_DIM), _F32))
    carry = fold(init, [own] + scores, [vb] + values)
    ngroups = (i + MOBA_GROUP - 1) // MOBA_GROUP
    m, l, acc = lax.fori_loop(1, ngroups, lambda jg, c: fold(c, *past_group(jg)), carry)
    o_ref[0] = (acc / l).astype(o_ref.dtype)


def _moba_attention(proj3, kmean_pad, rel_bias, col0):
    b, s, _ = proj3.shape
    nq = s // MOBA_BLOCK
    assert nq % MOBA_GROUP == 0 and nq <= LANES
    bkt = jnp.asarray(_rel_bucket_table())
    return pl.pallas_call(
        functools.partial(_moba_kernel, scale=HEAD_DIM ** -0.5),
        grid=(MOBA_HEADS, b, nq),
        in_specs=[pl.BlockSpec(memory_space=pltpu.SMEM),
                  pl.BlockSpec((1, MOBA_BLOCK, HEAD_DIM), lambda h, bi, i: (bi, i, col0 + h)),
                  pl.BlockSpec((1, s, HEAD_DIM), lambda h, bi, i: (bi, 0, col0 + MOBA_HEADS + h)),
                  pl.BlockSpec((1, s, HEAD_DIM), lambda h, bi, i: (bi, 0, col0 + 2 * MOBA_HEADS + h)),
                  pl.BlockSpec((1, LANES, HEAD_DIM), lambda h, bi, i: (bi, 0, h)),
                  pl.BlockSpec((BIAS_TILES, MOBA_BLOCK, MOBA_BLOCK), lambda h, bi, i: (0, 0, 0))],
        out_specs=pl.BlockSpec((1, MOBA_BLOCK, HEAD_DIM), lambda h, bi, i: (bi, i, h)),
        out_shape=jax.ShapeDtypeStruct((b, s, MOBA_HEADS * HEAD_DIM), _BF16),
        scratch_shapes=[pltpu.VMEM((BIAS_TILES + 1, MOBA_BLOCK, MOBA_BLOCK), _F32)],
        compiler_params=_params(("arbitrary", "arbitrary", "arbitrary")),
        name="moba_attn",
    )(rel_bias, proj3, proj3, proj3, kmean_pad, bkt)


def _mix_kernel(x_ref, osb_ref, omb_ref, wg1_ref, wg2_ref, b1_ref, b2_ref, wsb_ref, wmb_ref, o_ref):
    x = x_ref[...]
    g_sb = jax.nn.sigmoid(_dot(x, wg1_ref[...]) + b1_ref[...])
    g_mb = jax.nn.sigmoid(_dot(x, wg2_ref[...]) + b2_ref[...])
    y_sb = _dot(osb_ref[...], wsb_ref[...])
    y_mb = _dot(omb_ref[...], wmb_ref[...])
    o_ref[...] = (g_sb * y_sb + g_mb * y_mb).astype(o_ref.dtype)


def _mix(x_bf, o_sb, o_mb, wg_bf, b_gate, wsb_bf, wmb_bf, tm=1024, tn=512):
    t, d = x_bf.shape
    w = o_sb.shape[1]
    nj = d // tn
    b2d = b_gate.reshape(1, 2 * d)
    return pl.pallas_call(
        _mix_kernel,
        grid=(t // tm, nj),
        in_specs=[pl.BlockSpec((tm, d), lambda i, j: (i, 0)),
                  pl.BlockSpec((tm, w), lambda i, j: (i, 0)),
                  pl.BlockSpec((tm, w), lambda i, j: (i, 0)),
                  pl.BlockSpec((d, tn), lambda i, j: (0, j)),
                  pl.BlockSpec((d, tn), lambda i, j: (0, nj + j)),
                  pl.BlockSpec((1, tn), lambda i, j: (0, j)),
                  pl.BlockSpec((1, tn), lambda i, j: (0, nj + j)),
                  pl.BlockSpec((w, tn), lambda i, j: (0, j)),
                  pl.BlockSpec((w, tn), lambda i, j: (0, j))],
        out_specs=pl.BlockSpec((tm, tn), lambda i, j: (i, j)),
        out_shape=jax.ShapeDtypeStruct((t, d), _BF16),
        compiler_params=_params(("parallel", "arbitrary")),
        name="mix",
    )(x_bf, o_sb, o_mb, wg_bf, wg_bf, b2d, b2d, wsb_bf, wmb_bf)


def _layer_norm(h, g, b):
    mu = jnp.mean(h, axis=-1, keepdims=True)
    xc = h - mu
    var = jnp.mean(xc * xc, axis=-1, keepdims=True)
    return xc * lax.rsqrt(var + LN_EPS) * g + b


def _outln_kernel(m_ref, w_ref, x_ref, g_ref, b_ref, o_ref, obf_ref):
    h = DN_ALPHA * x_ref[...] + _dot(m_ref[...], w_ref[...])
    y = _layer_norm(h, g_ref[...], b_ref[...])
    o_ref[...] = y
    obf_ref[...] = y.astype(_BF16)


def _outln(mix_bf, w_bf, x2, g, b, tm=256):
    t, d = x2.shape
    row = pl.BlockSpec((tm, d), lambda i: (i, 0))
    vec = pl.BlockSpec((1, d), lambda i: (0, 0))
    return pl.pallas_call(
        _outln_kernel,
        grid=(t // tm,),
        in_specs=[row, pl.BlockSpec((d, d), lambda i: (0, 0)), row, vec, vec],
        out_specs=[row, row],
        out_shape=[jax.ShapeDtypeStruct((t, d), _F32), jax.ShapeDtypeStruct((t, d), _BF16)],
        compiler_params=_params(("parallel",)),
        name="outln",
    )(mix_bf, w_bf, x2, g.reshape(1, d), b.reshape(1, d))


def _pkeys_kernel(x_ref, w_ref, sk_ref, idx_ref, g_ref):
    tm = x_ref.shape[0]
    k = PEER_TOPK
    nk = PEER_NKEYS
    half = PEER_QDIM // 2
    inf = jnp.inf
    pq = _dot(x_ref[...], w_ref[...]).astype(_BF16)
    key_id = lax.broadcasted_iota(jnp.int32, (nk, tm), 0).astype(_F32)
    rank = lax.broadcasted_iota(jnp.int32, (k, tm), 0)

    def first_max(pieces, ids):
        m = pieces[0]
        for piece in pieces[1:]:
            m = jnp.maximum(m, piece)
        m = jnp.max(m, axis=0, keepdims=True)
        f = None
        for piece, pid in zip(pieces, ids):
            c = jnp.where(piece == m, pid, 1e9)
            f = c if f is None else jnp.minimum(f, c)
        return m, jnp.min(f, axis=0, keepdims=True)

    top_val, top_key = [], []
    for p in range(2):
        s = _nt_dot(sk_ref[p].astype(_BF16), pq[:, p * half:(p + 1) * half])
        val = jnp.zeros((k, tm), _F32)
        key = jnp.zeros((k, tm), _F32)
        for r in range(k):
            m, f = first_max([s], [key_id])
            val = jnp.where(rank == r, m, val)
            key = jnp.where(rank == r, f, key)
            s = jnp.where(key_id == f, -inf, s)
        top_val.append(val)
        top_key.append(key)

    row8 = lax.broadcasted_iota(jnp.int32, (8, tm), 0)
    row8_f = row8.astype(_F32)
    cand, expert, flat = [], [], []
    for a, b0 in [(0, 0), (0, 8)] + [(a, 0) for a in range(1, 8)]:
        val = top_val[0][a:a + 1] + top_val[1][b0:b0 + 8]
        if k // (a + 1) < b0 + 8:
            val = jnp.where(row8 + b0 < k // (a + 1), val, -inf)
        cand.append(val)
        expert.append(top_key[0][a:a + 1] * float(nk) + top_key[1][b0:b0 + 8])
        flat.append(row8_f + float(a * k + b0))
    cand.append(top_val[0][8:k] + top_val[1][0:1])
    expert.append(top_key[0][8:k] * float(nk) + top_key[1][0:1])
    flat.append((row8_f + 8.0) * float(k))
    best = jnp.zeros((k, tm), _F32)
    ids = jnp.zeros((k, tm), _F32)
    for r in range(k):
        m, f = first_max(cand, flat)
        e = None
        for a in range(len(cand)):
            hit = flat[a] == f
            ea = jnp.where(hit, expert[a], -1.0)
            e = ea if e is None else jnp.maximum(e, ea)
            cand[a] = jnp.where(hit, -inf, cand[a])
        best = jnp.where(rank == r, m, best)
        ids = jnp.where(rank == r, jnp.max(e, axis=0, keepdims=True), ids)
    w = jnp.exp(best - best[0:1])
    idx_ref[...] = ids.astype(jnp.int32)
    g_ref[...] = w / jnp.sum(w, axis=0, keepdims=True)


def _pkeys(x1_bf, wq_bf, sub_keys, tm=256):
    t, d = x1_bf.shape
    out = pl.BlockSpec((PEER_TOPK, tm), lambda i, h: (h, i))
    return pl.pallas_call(
        _pkeys_kernel,
        grid=(t // tm, PEER_HEADS),
        in_specs=[pl.BlockSpec((tm, d), lambda i, h: (i, 0)),
                  pl.BlockSpec((d, PEER_QDIM), lambda i, h: (0, h)),
                  pl.BlockSpec((2, PEER_NKEYS, PEER_QDIM // 2), lambda i, h: (0, 0, 0))],
        out_specs=[out, out],
        out_shape=[jax.ShapeDtypeStruct((PEER_HEADS * PEER_TOPK, t), jnp.int32),
                   jax.ShapeDtypeStruct((PEER_HEADS * PEER_TOPK, t), _F32)],
        compiler_params=_params(("parallel", "arbitrary")),
        name="pkeys",
    )(x1_bf, wq_bf, sub_keys)


def _peer_kernel(idx_cur, idx_one, idx_two, g_ref, x_ref, xp_ref, uv_hbm, grp_ref, grpt_ref, diag_ref,
                 lg_ref, lb_ref, o_ref, buf0, buf1, buf2, buf3, wl0, wl1, wl2, wl3, arow, sem):
    nexp = PEER_HEADS * PEER_TOPK
    sub = uv_hbm.shape[1] // 2
    nchunk = nexp // PEER_CHUNK
    s = pl.program_id(0)
    last = pl.num_programs(0) - 1
    bufs = (buf0, buf1, buf2, buf3)
    wls = (wl0, wl1, wl2, wl3)

    def start(slot, idx_ref, t, e):
        src = uv_hbm.at[idx_ref[t * nexp + e]]
        pltpu.make_async_copy(src, bufs[slot].at[t * nexp + e], sem.at[slot, t]).start(priority=e % 2)

    def wait(slot, t):
        pltpu.make_async_copy(uv_hbm.at[pl.ds(0, nexp)], bufs[slot].at[pl.ds(t * nexp, nexp)],
                              sem.at[slot, t]).wait()

    @pl.when(s == 0)
    def _():
        wl3[...] = jnp.zeros_like(wl3)

        def fill(t, carry):
            buf3[pl.ds(t * nexp, nexp)] = jnp.zeros((nexp,) + buf3.shape[1:], buf3.dtype)
            for e in range(nexp):
                start(0, idx_cur, t, e)
            for e in range(nexp):
                start(1, idx_one, t, e)
            return carry

        lax.fori_loop(0, PEER_TOK, fill, 0)

    def step(cur):
        nxt = (cur + 2) % PEER_RING
        prv = (cur + PEER_RING - 1) % PEER_RING
        diag = diag_ref[...]
        rows = []
        pending = []

        def flush(keep):
            while len(pending) > keep:
                t0, c0, val = pending.pop(0)
                arow[t0:t0 + 1, c0 * PEER_CHUNK * sub:(c0 + 1) * PEER_CHUNK * sub] = val

        todo = [(t, e) for t in range(PEER_TOK) for e in range(nexp)]

        def issue(count):
            for t, e in todo[:count]:
                start(nxt, idx_two, t, e)
            del todo[:count]

        for t in range(PEER_TOK):
            wait(cur, t)
        for t in range(PEER_TOK):
            xb = x_ref[t].astype(_BF16)
            for c in range(nchunk):
                base = t * nexp + c * PEER_CHUNK
                flush(PEER_STORE_LAG)
                issue(PEER_ISSUE[0])
                u = bufs[cur][base:base + PEER_CHUNK, 0:sub, :].reshape(PEER_CHUNK * sub, LANES)
                part = _nt_dot(xb, u) * diag
                pending.append((t, c, jnp.sum(part, axis=0, keepdims=True)))

        flush(0)
        issue(PEER_ISSUE[1])
        a = arow[...]
        a_hi = a.astype(_BF16)
        a_lo = (a - a_hi.astype(_F32)).astype(_BF16)
        act = _dot(a_hi, grp_ref[...]) + _dot(a_lo, grp_ref[...])
        gelu = 0.5 * act * (1.0 + lax.erf(act * (2.0 ** -0.5)))
        wls[cur][...] = _dot((g_ref[...] * gelu).astype(_BF16), grpt_ref[...])

        per_chunk = len(todo) // (PEER_TOK * nchunk)
        for t in range(PEER_TOK):
            total = jnp.zeros((sub, LANES), _F32)
            for c in range(nchunk):
                base = t * nexp + c * PEER_CHUNK
                cols = slice(c * PEER_CHUNK * sub, (c + 1) * PEER_CHUNK * sub)
                issue(per_chunk if (t, c) != (PEER_TOK - 1, nchunk - 1) else len(todo))
                v = bufs[prv][base:base + PEER_CHUNK, sub:2 * sub, :].reshape(PEER_CHUNK * sub, LANES)
                total = total + _dot((wls[prv][t:t + 1, cols] * diag).astype(_BF16), v)
            rows.append(DN_ALPHA * xp_ref[t] + total)
        h = jnp.stack(rows)
        inv_d = 1.0 / (sub * LANES)
        mu = jnp.sum(jnp.sum(h, axis=2, keepdims=True), axis=1, keepdims=True) * inv_d
        xc = h - mu
        var = jnp.sum(jnp.sum(xc * xc, axis=2, keepdims=True), axis=1, keepdims=True) * inv_d
        o_ref[...] = xc * lax.rsqrt(var + LN_EPS) * lg_ref[...] + lb_ref[...]

        @pl.when(s == last)
        def _():
            for slot in ((cur + 1) % PEER_RING, nxt):
                for t in range(PEER_TOK):
                    wait(slot, t)

    for cur in range(PEER_RING):
        pl.when(lax.rem(s, PEER_RING) == cur)(functools.partial(step, cur))


def _peer(idx_flat, g, x1r, uv, ln_g, ln_b):
    t, sub, _ = x1r.shape
    nexp = PEER_HEADS * PEER_TOPK
    n = t // PEER_TOK
    pos = np.arange(nexp * sub)
    grp = jnp.asarray(pos[:, None] // sub == np.arange(nexp)[None, :], _BF16)
    diag = jnp.asarray(np.arange(sub)[:, None] == np.arange(PEER_CHUNK * sub)[None, :] % sub, _F32)
    const2 = lambda shape: pl.BlockSpec(shape, lambda i: (0, 0))
    this = lambda i: jnp.minimum(i, n - 1)
    prev = lambda i: jnp.maximum(i - 1, 0)
    ahead = lambda k: (lambda i: jnp.minimum(i + k, n - 1))
    tok3 = lambda at: pl.BlockSpec((PEER_TOK, sub, LANES), lambda i: (at(i), 0, 0))
    ids = lambda at: pl.BlockSpec((PEER_TOK * nexp,), lambda i: (at(i),), memory_space=pltpu.SMEM)
    return pl.pallas_call(
        _peer_kernel,
        grid=(n + 1,),
        in_specs=[ids(this), ids(ahead(1)), ids(ahead(2)),
                  pl.BlockSpec((PEER_TOK, nexp), lambda i: (this(i), 0)),
                  tok3(this), tok3(prev),
                  pl.BlockSpec(memory_space=pl.ANY),
                  const2((nexp * sub, nexp)),
                  const2((nexp, nexp * sub)),
                  const2((sub, PEER_CHUNK * sub)),
                  const2((sub, LANES)),
                  const2((sub, LANES))],
        out_specs=tok3(prev),
        out_shape=jax.ShapeDtypeStruct((t, sub, LANES), _F32),
        scratch_shapes=[pltpu.VMEM((PEER_TOK * nexp, 2 * sub, LANES), _BF16)] * PEER_RING
        + [pltpu.VMEM((PEER_TOK, nexp * sub), _F32)] * (PEER_RING + 1)
        + [pltpu.SemaphoreType.DMA((PEER_RING, PEER_TOK))],
        compiler_params=_params(("arbitrary",)),
        name="peer",
    )(idx_flat, idx_flat, idx_flat, g, x1r, x1r, uv, grp, grp.T, diag,
      ln_g.reshape(sub, LANES), ln_b.reshape(sub, LANES))


def kernel(x, w_in, w_gate, b_gate, w_branch_sb, w_branch_moba, w_out, rel_bias, ln1_g, ln1_b,
           w_peer_query, peer_sub_keys, peer_u, peer_v, ln2_g, ln2_b):
    b, s, d = x.shape
    t = b * s
    x2 = x.reshape(t, d)
    x_bf = x2.astype(_BF16)
    sb_w = SB_HEADS * HEAD_DIM
    mb_w = MOBA_HEADS * HEAD_DIM

    proj, colmean = _proj(x_bf, w_in.astype(_BF16))
    proj3 = proj.reshape(b, s, proj.shape[1])
    nblk = s // MOBA_BLOCK
    kmean = colmean.reshape(b, nblk, -1)[:, :, 3 * sb_w + mb_w:3 * sb_w + 2 * mb_w]
    kmean_pad = jnp.pad(kmean, ((0, 0), (0, LANES - nblk), (0, 0)))

    row = np.arange(SB_BLOCK)[:, None]
    col = np.arange(2 * SB_BLOCK)[None, :]
    tri = jnp.asarray((row > col) | (col >= SB_BLOCK), _BF16)

    o_sb = _sb_attention(proj3, tri, 0)
    o_mb = _moba_attention(proj3, kmean_pad, rel_bias, 3 * sb_w // HEAD_DIM)
    mix = _mix(x_bf, o_sb.reshape(t, sb_w), o_mb.reshape(t, mb_w), w_gate.astype(_BF16), b_gate,
               w_branch_sb.astype(_BF16), w_branch_moba.astype(_BF16))
    x1, x1_bf = _outln(mix, w_out.astype(_BF16), x2, ln1_g, ln1_b)

    idx_t, g_t = _pkeys(x1_bf, w_peer_query.astype(_BF16), peer_sub_keys)
    sub = d // LANES
    n_exp = peer_u.shape[0]
    uv = jnp.concatenate([peer_u.reshape(n_exp, sub, LANES), peer_v.reshape(n_exp, sub, LANES)],
                         axis=1).astype(_BF16)
    out = _peer(idx_t.T.reshape(-1), g_t.T, x1.reshape(t, sub, LANES), uv, ln2_g, ln2_b)
    return out.reshape(b, s, d)
```

```python
import functools
import math

import numpy as np
import jax
import jax.numpy as jnp
from jax import lax
from jax.experimental import pallas as pl
from jax.experimental.pallas import tpu as pltpu

HEAD_DIM = 128
SB_HEADS = 8
MOBA_HEADS = 8
SB_BLOCK = 128
SB_TQ = 512
SB_GROUP = 1
SB_DEAD_LOG = -110.0
MOBA_BLOCK = 256
MOBA_TOPK = 3
MOBA_GROUP = 4
REL_BUCKETS = 32
REL_MAX_DIST = 1024
BIAS_TILES = 5
PEER_HEADS = 8
PEER_NKEYS = 128
PEER_QDIM = 256
PEER_TOPK = 16
LN_EPS = 1e-5
DN_ALPHA = 2.0 ** 0.25
NEG = -1e30
LANES = 128
PEER_TOK = 8
PEER_CHUNK = 16
PEER_RING = 4
PEER_STORE_LAG = 3
PEER_ISSUE = (8, 64)
VMEM_LIMIT = 56 * 1024 * 1024

_F32 = jnp.float32
_BF16 = jnp.bfloat16


def _nt_dot(a, b):
    return lax.dot_general(a, b, (((1,), (1,)), ((), ())), preferred_element_type=_F32)


def _dot(a, b):
    return jnp.dot(a, b, preferred_element_type=_F32)


def _params(sem):
    return pltpu.CompilerParams(dimension_semantics=sem, vmem_limit_bytes=VMEM_LIMIT)


def _proj_kernel(x_ref, w_ref, o_ref, m_ref):
    acc = _dot(x_ref[...], w_ref[...])
    o_ref[...] = acc.astype(o_ref.dtype)
    tm, tn = acc.shape
    nb = tm // MOBA_BLOCK
    m_ref[...] = jnp.mean(acc.reshape(nb, MOBA_BLOCK, tn), axis=1)[:, None, :]


def _proj(x_bf, w_bf, tm=1024, tn=512):
    t, d = x_bf.shape
    n = w_bf.shape[1]
    nb = tm // MOBA_BLOCK
    return pl.pallas_call(
        _proj_kernel,
        grid=(t // tm, n // tn),
        in_specs=[pl.BlockSpec((tm, d), lambda i, j: (i, 0)),
                  pl.BlockSpec((d, tn), lambda i, j: (0, j))],
        out_specs=[pl.BlockSpec((tm, tn), lambda i, j: (i, j)),
                   pl.BlockSpec((nb, 1, tn), lambda i, j: (i, 0, j))],
        out_shape=[jax.ShapeDtypeStruct((t, n), _BF16),
                   jax.ShapeDtypeStruct((t // MOBA_BLOCK, 1, n), _F32)],
        compiler_params=_params(("parallel", "arbitrary")),
        name="proj",
    )(x_bf, w_bf)


def _sb_kernel(q_ref, k_ref, v_ref, tri_ref, o_ref, c_ref, acc_ref, *, scale):
    i = pl.program_id(2)
    nsub = SB_TQ // SB_BLOCK
    tri = tri_ref[...]
    row = lax.broadcasted_iota(jnp.int32, (SB_TQ, SB_BLOCK), 0)
    col = lax.broadcasted_iota(jnp.int32, (SB_TQ, SB_BLOCK), 1)

    def blocks(g, j_top, causals):
        lanes = slice(g * HEAD_DIM, (g + 1) * HEAD_DIM)
        q = q_ref[0, :, lanes]
        parts = []
        for u, causal in enumerate(causals):
            start = pl.multiple_of((j_top - u) * SB_BLOCK, SB_BLOCK)
            k = k_ref[0, pl.ds(start, SB_BLOCK), lanes]
            z = _nt_dot(q, k) * scale
            lg = -(jnp.maximum(z, 0.0) + jnp.log1p(jnp.exp(-jnp.abs(z))))
            if causal is not None:
                lg = jnp.where(causal, lg, 0.0)
            lg_hi = lg.astype(_BF16)
            lg_lo = (lg - lg_hi.astype(_F32)).astype(_BF16)
            tr = _dot(lg_hi, tri) + _dot(lg_lo, tri)
            parts.append((start, z + lg, tr, causal))
        c = c_ref[g]
        pv = None
        for start, log_beta, tr, causal in parts:
            w = jnp.exp(log_beta + tr[:, :SB_BLOCK] + c)
            if causal is not None:
                w = jnp.where(causal, w, 0.0)
            d = _dot(w.astype(_BF16), v_ref[0, pl.ds(start, SB_BLOCK), lanes])
            pv = d if pv is None else pv + d
            c = c + tr[:, SB_BLOCK:]
        c_ref[g] = c
        acc_ref[g] += pv

    c_ref[...] = jnp.zeros_like(c_ref)
    acc_ref[...] = jnp.zeros_like(acc_ref)

    diag_masks = [col + dj * SB_BLOCK < row for dj in range(nsub - 1, -1, -1)]
    for g in range(SB_GROUP):
        blocks(g, nsub * i + nsub - 1, diag_masks)

    def live(carry):
        jj, log_rest = carry
        return jnp.logical_and(jj < i, log_rest > SB_DEAD_LOG)

    def body(carry):
        jj, _ = carry
        for g in range(SB_GROUP):
            blocks(g, nsub * (i - jj) - 1, [None] * nsub)
        return jj + 1, jnp.max(c_ref[...])

    lax.while_loop(live, body, (jnp.int32(0), jnp.max(c_ref[...])))
    for g in range(SB_GROUP):
        o_ref[0, :, g * HEAD_DIM:(g + 1) * HEAD_DIM] = acc_ref[g].astype(o_ref.dtype)


def _sb_attention(proj3, tri, col0):
    b, s, _ = proj3.shape
    nq = s // SB_TQ
    gw = SB_GROUP * HEAD_DIM
    ng = SB_HEADS // SB_GROUP
    c0 = col0 // SB_GROUP
    return pl.pallas_call(
        functools.partial(_sb_kernel, scale=HEAD_DIM ** -0.5),
        grid=(b, ng, nq),
        in_specs=[pl.BlockSpec((1, SB_TQ, gw), lambda bi, h, i: (bi, i, c0 + h)),
                  pl.BlockSpec((1, s, gw), lambda bi, h, i: (bi, 0, c0 + ng + h)),
                  pl.BlockSpec((1, s, gw), lambda bi, h, i: (bi, 0, c0 + 2 * ng + h)),
                  pl.BlockSpec((SB_BLOCK, 2 * SB_BLOCK), lambda bi, h, i: (0, 0))],
        out_specs=pl.BlockSpec((1, SB_TQ, gw), lambda bi, h, i: (bi, i, h)),
        out_shape=jax.ShapeDtypeStruct((b, s, SB_HEADS * HEAD_DIM), _BF16),
        scratch_shapes=[pltpu.VMEM((SB_GROUP, SB_TQ, HEAD_DIM), _F32),
                        pltpu.VMEM((SB_GROUP, SB_TQ, HEAD_DIM), _F32)],
        compiler_params=_params(("parallel", "parallel", "arbitrary")),
        name="sb_attn",
    )(proj3, proj3, proj3, tri)


def _rel_bucket_table():
    n_exact = REL_BUCKETS // 2
    d = np.arange(BIAS_TILES)[:, None, None]
    q = np.arange(MOBA_BLOCK)[None, :, None]
    k = np.arange(MOBA_BLOCK)[None, None, :]
    rel = np.maximum(d * MOBA_BLOCK + q - k, 0)
    logd = (np.log(np.maximum(rel, 1).astype(np.float32) / np.float32(n_exact))
            / np.float32(math.log(REL_MAX_DIST / n_exact))).astype(np.float32)
    large = n_exact + (logd * np.float32(REL_BUCKETS - n_exact)).astype(np.int32)
    large = np.minimum(large, REL_BUCKETS - 1)
    return np.where(rel < n_exact, rel, large).astype(np.int32)


def _moba_kernel(rb_ref, q_ref, k_ref, v_ref, km_ref, bkt_ref, o_ref, bias_ref, *, scale):
    h = pl.program_id(0)
    first = jnp.logical_and(pl.program_id(1) == 0, pl.program_id(2) == 0)
    i = pl.program_id(2)

    @pl.when(first)
    def _():
        for d in range(BIAS_TILES):
            bkt = bkt_ref[d]
            tile = jnp.zeros((MOBA_BLOCK, MOBA_BLOCK), _F32)
            for bi in range(REL_BUCKETS):
                tile = jnp.where(bkt == bi, rb_ref[h, bi], tile)
            bias_ref[d] = tile
        bias_ref[BIAS_TILES] = jnp.full((MOBA_BLOCK, MOBA_BLOCK), rb_ref[h, REL_BUCKETS - 1], _F32)

    q = q_ref[0]
    lane = lax.broadcasted_iota(jnp.int32, (MOBA_BLOCK, LANES), 1)
    valid = lane < i
    gate = _nt_dot(q, km_ref[0].astype(_BF16))
    g = jnp.where(valid, gate, -jnp.inf)
    sel = jnp.zeros((MOBA_BLOCK, LANES), jnp.bool_)
    for _ in range(MOBA_TOPK):
        m = jnp.max(g, axis=-1, keepdims=True)
        first_idx = jnp.min(jnp.where(g == m, lane, LANES), axis=-1, keepdims=True)
        pick = lane == first_idx
        sel = jnp.logical_or(sel, pick)
        g = jnp.where(pick, -jnp.inf, g)
    sel_f = jnp.where(jnp.logical_and(sel, valid), 1.0, 0.0)

    row = lax.broadcasted_iota(jnp.int32, (MOBA_BLOCK, MOBA_BLOCK), 0)
    col = lax.broadcasted_iota(jnp.int32, (MOBA_BLOCK, MOBA_BLOCK), 1)

    def block(j):
        start = pl.multiple_of(j * MOBA_BLOCK, MOBA_BLOCK)
        return k_ref[0, pl.ds(start, MOBA_BLOCK), :], v_ref[0, pl.ds(start, MOBA_BLOCK), :]

    def past_group(jg):
        scores, values = [], []
        for u in range(MOBA_GROUP):
            j = jg * MOBA_GROUP + u
            kb, vb = block(j)
            d = jnp.clip(i - j, 0, BIAS_TILES)
            s = _nt_dot(q, kb) * scale + bias_ref[d]
            chosen = jnp.sum(jnp.where(lane == j, sel_f, 0.0), axis=-1, keepdims=True)
            scores.append(jnp.where(chosen > 0.5, s, NEG))
            values.append(vb)
        return scores, values

    def fold(carry, scores, values):
        m, l, acc = carry
        top = functools.reduce(jnp.maximum, scores)
        m_new = jnp.maximum(m, jnp.max(top, axis=-1, keepdims=True))
        a = jnp.exp(m - m_new)
        probs = [jnp.exp(s - m_new) for s in scores]
        l = a * l + jnp.sum(functools.reduce(jnp.add, probs), axis=-1, keepdims=True)
        acc = a * acc + functools.reduce(jnp.add, [_dot(p.astype(_BF16), vb) for p, vb in zip(probs, values)])
        return m_new, l, acc

    kb, vb = block(i)
    own = jnp.where(col <= row, _nt_dot(q, kb) * scale + bias_ref[0], NEG)
    scores, values = past_group(0)
    init = (jnp.full((MOBA_BLOCK, 1), NEG, _F32), jnp.zeros((MOBA_BLOCK, 1), _F32),
            jnp.zeros((MOBA_BLOCK, HEAD_DIM), _F32))
    carry = fold(init, [own] + scores, [vb] + values)
    ngroups = (i + MOBA_GROUP - 1) // MOBA_GROUP
    m, l, acc = lax.fori_loop(1, ngroups, lambda jg, c: fold(c, *past_group(jg)), carry)
    o_ref[0] = (acc / l).astype(o_ref.dtype)


def _moba_attention(proj3, kmean_pad, rel_bias, col0):
    b, s, _ = proj3.shape
    nq = s // MOBA_BLOCK
    assert nq % MOBA_GROUP == 0 and nq <= LANES
    bkt = jnp.asarray(_rel_bucket_table())
    return pl.pallas_call(
        functools.partial(_moba_kernel, scale=HEAD_DIM ** -0.5),
        grid=(MOBA_HEADS, b, nq),
        in_specs=[pl.BlockSpec(memory_space=pltpu.SMEM),
                  pl.BlockSpec((1, MOBA_BLOCK, HEAD_DIM), lambda h, bi, i: (bi, i, col0 + h)),
                  pl.BlockSpec((1, s, HEAD_DIM), lambda h, bi, i: (bi, 0, col0 + MOBA_HEADS + h)),
                  pl.BlockSpec((1, s, HEAD_DIM), lambda h, bi, i: (bi, 0, col0 + 2 * MOBA_HEADS + h)),
                  pl.BlockSpec((1, LANES, HEAD_DIM), lambda h, bi, i: (bi, 0, h)),
                  pl.BlockSpec((BIAS_TILES, MOBA_BLOCK, MOBA_BLOCK), lambda h, bi, i: (0, 0, 0))],
        out_specs=pl.BlockSpec((1, MOBA_BLOCK, HEAD_DIM), lambda h, bi, i: (bi, i, h)),
        out_shape=jax.ShapeDtypeStruct((b, s, MOBA_HEADS * HEAD_DIM), _BF16),
        scratch_shapes=[pltpu.VMEM((BIAS_TILES + 1, MOBA_BLOCK, MOBA_BLOCK), _F32)],
        compiler_params=_params(("arbitrary", "arbitrary", "arbitrary")),
        name="moba_attn",
    )(rel_bias, proj3, proj3, proj3, kmean_pad, bkt)


def _mix_kernel(x_ref, osb_ref, omb_ref, wg1_ref, wg2_ref, b1_ref, b2_ref, wsb_ref, wmb_ref, o_ref):
    x = x_ref[...]
    g_sb = jax.nn.sigmoid(_dot(x, wg1_ref[...]) + b1_ref[...])
    g_mb = jax.nn.sigmoid(_dot(x, wg2_ref[...]) + b2_ref[...])
    y_sb = _dot(osb_ref[...], wsb_ref[...])
    y_mb = _dot(omb_ref[...], wmb_ref[...])
    o_ref[...] = (g_sb * y_sb + g_mb * y_mb).astype(o_ref.dtype)


def _mix(x_bf, o_sb, o_mb, wg_bf, b_gate, wsb_bf, wmb_bf, tm=1024, tn=512):
    t, d = x_bf.shape
    w = o_sb.shape[1]
    nj = d // tn
    b2d = b_gate.reshape(1, 2 * d)
    return pl.pallas_call(
        _mix_kernel,
        grid=(t // tm, nj),
        in_specs=[pl.BlockSpec((tm, d), lambda i, j: (i, 0)),
                  pl.BlockSpec((tm, w), lambda i, j: (i, 0)),
                  pl.BlockSpec((tm, w), lambda i, j: (i, 0)),
                  pl.BlockSpec((d, tn), lambda i, j: (0, j)),
                  pl.BlockSpec((d, tn), lambda i, j: (0, nj + j)),
                  pl.BlockSpec((1, tn), lambda i, j: (0, j)),
                  pl.BlockSpec((1, tn), lambda i, j: (0, nj + j)),
                  pl.BlockSpec((w, tn), lambda i, j: (0, j)),
                  pl.BlockSpec((w, tn), lambda i, j: (0, j))],
        out_specs=pl.BlockSpec((tm, tn), lambda i, j: (i, j)),
        out_shape=jax.ShapeDtypeStruct((t, d), _BF16),
        compiler_params=_params(("parallel", "arbitrary")),
        name="mix",
    )(x_bf, o_sb, o_mb, wg_bf, wg_bf, b2d, b2d, wsb_bf, wmb_bf)


def _layer_norm(h, g, b):
    mu = jnp.mean(h, axis=-1, keepdims=True)
    xc = h - mu
    var = jnp.mean(xc * xc, axis=-1, keepdims=True)
    return xc * lax.rsqrt(var + LN_EPS) * g + b


def _outln_kernel(m_ref, w_ref, x_ref, g_ref, b_ref, o_ref, obf_ref):
    h = DN_ALPHA * x_ref[...] + _dot(m_ref[...], w_ref[...])
    y = _layer_norm(h, g_ref[...], b_ref[...])
    o_ref[...] = y.reshape(o_ref.shape)
    obf_ref[...] = y.astype(_BF16)


def _outln(mix_bf, w_bf, x2, g, b, tm=256):
    t, d = x2.shape
    row = pl.BlockSpec((tm, d), lambda i: (i, 0))
    vec = pl.BlockSpec((1, d), lambda i: (0, 0))
    return pl.pallas_call(
        _outln_kernel,
        grid=(t // tm,),
        in_specs=[row, pl.BlockSpec((d, d), lambda i: (0, 0)), row, vec, vec],
        out_specs=[pl.BlockSpec((tm, d // LANES, LANES), lambda i: (i, 0, 0)), row],
        out_shape=[jax.ShapeDtypeStruct((t, d // LANES, LANES), _F32), jax.ShapeDtypeStruct((t, d), _BF16)],
        compiler_params=_params(("parallel",)),
        name="outln",
    )(mix_bf, w_bf, x2, g.reshape(1, d), b.reshape(1, d))


def _pkeys_kernel(x_ref, w_ref, sk_ref, idx_ref, g_ref):
    tm = x_ref.shape[0]
    k = PEER_TOPK
    nk = PEER_NKEYS
    half = PEER_QDIM // 2
    inf = jnp.inf
    pq = _dot(x_ref[...], w_ref[...]).astype(_BF16)
    key_id = lax.broadcasted_iota(jnp.int32, (nk, tm), 0).astype(_F32)
    rank = lax.broadcasted_iota(jnp.int32, (k, tm), 0)

    def first_max(pieces, ids):
        m = pieces[0]
        for piece in pieces[1:]:
            m = jnp.maximum(m, piece)
        m = jnp.max(m, axis=0, keepdims=True)
        f = None
        for piece, pid in zip(pieces, ids):
            c = jnp.where(piece == m, pid, 1e9)
            f = c if f is None else jnp.minimum(f, c)
        return m, jnp.min(f, axis=0, keepdims=True)

    top_val, top_key = [], []
    for p in range(2):
        s = _nt_dot(sk_ref[p].astype(_BF16), pq[:, p * half:(p + 1) * half])
        val = jnp.zeros((k, tm), _F32)
        key = jnp.zeros((k, tm), _F32)
        for r in range(k):
            m, f = first_max([s], [key_id])
            val = jnp.where(rank == r, m, val)
            key = jnp.where(rank == r, f, key)
            s = jnp.where(key_id == f, -inf, s)
        top_val.append(val)
        top_key.append(key)

    row8 = lax.broadcasted_iota(jnp.int32, (8, tm), 0)
    row8_f = row8.astype(_F32)
    cand, expert, flat = [], [], []
    for a, b0 in [(0, 0), (0, 8)] + [(a, 0) for a in range(1, 8)]:
        val = top_val[0][a:a + 1] + top_val[1][b0:b0 + 8]
        if k // (a + 1) < b0 + 8:
            val = jnp.where(row8 + b0 < k // (a + 1), val, -inf)
        cand.append(val)
        expert.append(top_key[0][a:a + 1] * float(nk) + top_key[1][b0:b0 + 8])
        flat.append(row8_f + float(a * k + b0))
    cand.append(top_val[0][8:k] + top_val[1][0:1])
    expert.append(top_key[0][8:k] * float(nk) + top_key[1][0:1])
    flat.append((row8_f + 8.0) * float(k))
    best = jnp.zeros((k, tm), _F32)
    ids = jnp.zeros((k, tm), _F32)
    for r in range(k):
        m, f = first_max(cand, flat)
        e = None
        for a in range(len(cand)):
            hit = flat[a] == f
            ea = jnp.where(hit, expert[a], -1.0)
            e = ea if e is None else jnp.maximum(e, ea)
            cand[a] = jnp.where(hit, -inf, cand[a])
        best = jnp.where(rank == r, m, best)
        ids = jnp.where(rank == r, jnp.max(e, axis=0, keepdims=True), ids)
    w = jnp.exp(best - best[0:1])
    idx_ref[...] = ids.astype(jnp.int32)
    g_ref[...] = w / jnp.sum(w, axis=0, keepdims=True)


def _pkeys(x1_bf, wq_bf, sub_keys, tm=256):
    t, d = x1_bf.shape
    out = pl.BlockSpec((PEER_TOPK, tm), lambda i, h: (h, i))
    return pl.pallas_call(
        _pkeys_kernel,
        grid=(t // tm, PEER_HEADS),
        in_specs=[pl.BlockSpec((tm, d), lambda i, h: (i, 0)),
                  pl.BlockSpec((d, PEER_QDIM), lambda i, h: (0, h)),
                  pl.BlockSpec((2, PEER_NKEYS, PEER_QDIM // 2), lambda i, h: (0, 0, 0))],
        out_specs=[out, out],
        out_shape=[jax.ShapeDtypeStruct((PEER_HEADS * PEER_TOPK, t), jnp.int32),
                   jax.ShapeDtypeStruct((PEER_HEADS * PEER_TOPK, t), _F32)],
        compiler_params=_params(("parallel", "arbitrary")),
        name="pkeys",
    )(x1_bf, wq_bf, sub_keys)


def _peer_kernel(idx_cur, idx_one, idx_two, g_ref, x_ref, xp_ref, uv_hbm, grp_ref, grpt_ref, diag_ref,
                 lg_ref, lb_ref, o_ref, buf0, buf1, buf2, buf3, wl0, wl1, wl2, wl3, arow, sem):
    nexp = PEER_HEADS * PEER_TOPK
    sub = uv_hbm.shape[1] // 2
    nchunk = nexp // PEER_CHUNK
    s = pl.program_id(0)
    last = pl.num_programs(0) - 1
    bufs = (buf0, buf1, buf2, buf3)
    wls = (wl0, wl1, wl2, wl3)

    def start(slot, idx_ref, t, e):
        src = uv_hbm.at[idx_ref[t * nexp + e]]
        pltpu.make_async_copy(src, bufs[slot].at[t * nexp + e], sem.at[slot, t]).start(priority=e % 2)

    def wait(slot, t):
        pltpu.make_async_copy(uv_hbm.at[pl.ds(0, nexp)], bufs[slot].at[pl.ds(t * nexp, nexp)],
                              sem.at[slot, t]).wait()

    @pl.when(s == 0)
    def _():
        wl3[...] = jnp.zeros_like(wl3)

        def fill(t, carry):
            buf3[pl.ds(t * nexp, nexp)] = jnp.zeros((nexp,) + buf3.shape[1:], buf3.dtype)
            for e in range(nexp):
                start(0, idx_cur, t, e)
            for e in range(nexp):
                start(1, idx_one, t, e)
            return carry

        lax.fori_loop(0, PEER_TOK, fill, 0)

    def step(cur):
        nxt = (cur + 2) % PEER_RING
        prv = (cur + PEER_RING - 1) % PEER_RING
        diag = diag_ref[...]
        rows = []
        pending = []

        def flush(keep):
            while len(pending) > keep:
                t0, c0, val = pending.pop(0)
                arow[t0:t0 + 1, c0 * PEER_CHUNK * sub:(c0 + 1) * PEER_CHUNK * sub] = val

        todo = [(t, e) for t in range(PEER_TOK) for e in range(nexp)]

        def issue(count):
            for t, e in todo[:count]:
                start(nxt, idx_two, t, e)
            del todo[:count]

        for t in range(PEER_TOK):
            wait(cur, t)
        for t in range(PEER_TOK):
            xb = x_ref[t].astype(_BF16)
            for c in range(nchunk):
                base = t * nexp + c * PEER_CHUNK
                flush(PEER_STORE_LAG)
                issue(PEER_ISSUE[0])
                u = bufs[cur][base:base + PEER_CHUNK, 0:sub, :].reshape(PEER_CHUNK * sub, LANES)
                part = _nt_dot(xb, u) * diag
                pending.append((t, c, jnp.sum(part, axis=0, keepdims=True)))

        flush(0)
        issue(PEER_ISSUE[1])
        a = arow[...]
        a_hi = a.astype(_BF16)
        a_lo = (a - a_hi.astype(_F32)).astype(_BF16)
        act = _dot(a_hi, grp_ref[...]) + _dot(a_lo, grp_ref[...])
        gelu = 0.5 * act * (1.0 + lax.erf(act * (2.0 ** -0.5)))
        wls[cur][...] = _dot((g_ref[...] * gelu).astype(_BF16), grpt_ref[...])

        per_chunk = len(todo) // (PEER_TOK * nchunk)
        for t in range(PEER_TOK):
            total = jnp.zeros((sub, LANES), _F32)
            for c in range(nchunk):
                base = t * nexp + c * PEER_CHUNK
                cols = slice(c * PEER_CHUNK * sub, (c + 1) * PEER_CHUNK * sub)
                issue(per_chunk if (t, c) != (PEER_TOK - 1, nchunk - 1) else len(todo))
                v = bufs[prv][base:base + PEER_CHUNK, sub:2 * sub, :].reshape(PEER_CHUNK * sub, LANES)
                total = total + _dot((wls[prv][t:t + 1, cols] * diag).astype(_BF16), v)
            rows.append(DN_ALPHA * xp_ref[t] + total)
        h = jnp.stack(rows)
        inv_d = 1.0 / (sub * LANES)
        mu = jnp.sum(jnp.sum(h, axis=2, keepdims=True), axis=1, keepdims=True) * inv_d
        xc = h - mu
        var = jnp.sum(jnp.sum(xc * xc, axis=2, keepdims=True), axis=1, keepdims=True) * inv_d
        o_ref[...] = xc * lax.rsqrt(var + LN_EPS) * lg_ref[...] + lb_ref[...]

        @pl.when(s == last)
        def _():
            for slot in ((cur + 1) % PEER_RING, nxt):
                for t in range(PEER_TOK):
                    wait(slot, t)

    for cur in range(PEER_RING):
        pl.when(lax.rem(s, PEER_RING) == cur)(functools.partial(step, cur))


def _peer(idx_flat, g, x1r, uv, ln_g, ln_b):
    t, sub, _ = x1r.shape
    nexp = PEER_HEADS * PEER_TOPK
    n = t // PEER_TOK
    pos = np.arange(nexp * sub)
    grp = jnp.asarray(pos[:, None] // sub == np.arange(nexp)[None, :], _BF16)
    diag = jnp.asarray(np.arange(sub)[:, None] == np.arange(PEER_CHUNK * sub)[None, :] % sub, _F32)
    const2 = lambda shape: pl.BlockSpec(shape, lambda i: (0, 0))
    this = lambda i: jnp.minimum(i, n - 1)
    prev = lambda i: jnp.maximum(i - 1, 0)
    ahead = lambda k: (lambda i: jnp.minimum(i + k, n - 1))
    tok3 = lambda at: pl.BlockSpec((PEER_TOK, sub, LANES), lambda i: (at(i), 0, 0))
    ids = lambda at: pl.BlockSpec((PEER_TOK * nexp,), lambda i: (at(i),), memory_space=pltpu.SMEM)
    return pl.pallas_call(
        _peer_kernel,
        grid=(n + 1,),
        in_specs=[ids(this), ids(ahead(1)), ids(ahead(2)),
                  pl.BlockSpec((PEER_TOK, nexp), lambda i: (this(i), 0)),
                  tok3(this), tok3(prev),
                  pl.BlockSpec(memory_space=pl.ANY),
                  const2((nexp * sub, nexp)),
                  const2((nexp, nexp * sub)),
                  const2((sub, PEER_CHUNK * sub)),
                  const2((sub, LANES)),
                  const2((sub, LANES))],
        out_specs=tok3(prev),
        out_shape=jax.ShapeDtypeStruct((t, sub, LANES), _F32),
        scratch_shapes=[pltpu.VMEM((PEER_TOK * nexp, 2 * sub, LANES), _BF16)] * PEER_RING
        + [pltpu.VMEM((PEER_TOK, nexp * sub), _F32)] * (PEER_RING + 1)
        + [pltpu.SemaphoreType.DMA((PEER_RING, PEER_TOK))],
        compiler_params=_params(("arbitrary",)),
        name="peer",
    )(idx_flat, idx_flat, idx_flat, g, x1r, x1r, uv, grp, grp.T, diag,
      ln_g.reshape(sub, LANES), ln_b.reshape(sub, LANES))


def kernel(x, w_in, w_gate, b_gate, w_branch_sb, w_branch_moba, w_out, rel_bias, ln1_g, ln1_b,
           w_peer_query, peer_sub_keys, peer_u, peer_v, ln2_g, ln2_b):
    b, s, d = x.shape
    t = b * s
    x2 = x.reshape(t, d)
    x_bf = x2.astype(_BF16)
    sb_w = SB_HEADS * HEAD_DIM
    mb_w = MOBA_HEADS * HEAD_DIM

    proj, colmean = _proj(x_bf, w_in.astype(_BF16))
    proj3 = proj.reshape(b, s, proj.shape[1])
    nblk = s // MOBA_BLOCK
    kmean = colmean.reshape(b, nblk, -1)[:, :, 3 * sb_w + mb_w:3 * sb_w + 2 * mb_w]
    kmean_pad = jnp.pad(kmean, ((0, 0), (0, LANES - nblk), (0, 0)))

    row = np.arange(SB_BLOCK)[:, None]
    col = np.arange(2 * SB_BLOCK)[None, :]
    tri = jnp.asarray((row > col) | (col >= SB_BLOCK), _BF16)

    o_sb = _sb_attention(proj3, tri, 0)
    o_mb = _moba_attention(proj3, kmean_pad, rel_bias, 3 * sb_w // HEAD_DIM)
    mix = _mix(x_bf, o_sb.reshape(t, sb_w), o_mb.reshape(t, mb_w), w_gate.astype(_BF16), b_gate,
               w_branch_sb.astype(_BF16), w_branch_moba.astype(_BF16))
    x1, x1_bf = _outln(mix, w_out.astype(_BF16), x2, ln1_g, ln1_b)

    idx_t, g_t = _pkeys(x1_bf, w_peer_query.astype(_BF16), peer_sub_keys)
    sub = d // LANES
    n_exp = peer_u.shape[0]
    uv = jnp.concatenate([peer_u.reshape(n_exp, sub, LANES), peer_v.reshape(n_exp, sub, LANES)],
                         axis=1).astype(_BF16)
    out = _peer(idx_t.T.reshape(-1), g_t.T, x1, uv, ln2_g, ln2_b)
    return out.reshape(b, s, d)
```
